```python
import math
import jax, jax.numpy as jnp
from jax import lax
import numpy as np

D_MODEL = 2048
BATCH = 2
SEQ = 4096
DEPTH = 4

N_MIXERS = 2
SSD_EXPAND = 2
D_INNER = SSD_EXPAND * D_MODEL
HEAD_DIM = 64
N_HEADS = D_INNER // HEAD_DIM
N_GROUPS = 8
HEADS_PER_GROUP = N_HEADS // N_GROUPS
D_STATE = 128
CONV_WIDTH = 5
CHUNK = 128
D_BC = N_GROUPS * D_STATE
D_XBC = D_INNER + 2 * D_BC
D_IN_PROJ = D_INNER + D_XBC + 2 * N_HEADS
DT_MIN = 1e-3
DT_MAX = 1e-1
A_INIT_MIN = 1.0
A_INIT_MAX = 16.0
POOL_WINDOWS = (2, 4, 8, 16)
N_POOL_GROUPS = len(POOL_WINDOWS)
POOL_GROUP_DIM = D_MODEL // N_POOL_GROUPS
D_FF = 4 * D_MODEL
DEEPNORM_ALPHA = (2.0 * DEPTH) ** 0.25
DEEPNORM_BETA = (8.0 * DEPTH) ** -0.25
LN_EPS = 1e-5
RMS_EPS = 1e-5
N_SSD_LAYERS = (DEPTH + 1) // 2
N_POOL_LAYERS = DEPTH // 2

kernel_name = "bidir_ssd_pool_hybrid_deepnorm"


def layer_norm(x, g, b):
    xf = x.astype(jnp.float32)
    mu = jnp.mean(xf, axis=-1, keepdims=True)
    var = jnp.mean(jnp.square(xf - mu), axis=-1, keepdims=True)
    y = (xf - mu) * lax.rsqrt(var + LN_EPS) * g.astype(jnp.float32) + b.astype(jnp.float32)
    return y.astype(x.dtype)


def centred_depthwise_conv(x, w, bias):
    c = x.shape[-1]
    pad = CONV_WIDTH // 2
    y = lax.conv_general_dilated(
        x, w, window_strides=(1,), padding=[(pad, pad)],
        dimension_numbers=("NWC", "WIO", "NWC"), feature_group_count=c)
    return y + bias


def ssd_chunked(x, dt, A, Bm, Cm):
    b, T, g, r, p = x.shape
    n = Bm.shape[-1]
    c = T // CHUNK
    f32 = jnp.float32
    xdt = (x.astype(f32) * dt[..., None]).reshape(b, c, CHUNK, g, r, p)
    a_cs = jnp.cumsum((dt * A).reshape(b, c, CHUNK, g, r), axis=2)
    Bc = Bm.astype(f32).reshape(b, c, CHUNK, g, n)
    Cc = Cm.astype(f32).reshape(b, c, CHUNK, g, n)
    mask = jnp.tril(jnp.ones((CHUNK, CHUNK), dtype=bool))[None, None, :, :, None, None]
    seg = a_cs[:, :, :, None] - a_cs[:, :, None, :]
    decay = jnp.exp(jnp.where(mask, seg, -jnp.inf))
    scores = jnp.einsum("bclgn,bcsgn->bclsg", Cc, Bc)
    y_diag = jnp.einsum("bclsg,bclsgr,bcsgrp->bclgrp", scores, decay, xdt)
    decay_to_end = jnp.exp(a_cs[:, :, -1:] - a_cs)
    states = jnp.einsum("bclgn,bclgr,bclgrp->bcgrpn", Bc, decay_to_end, xdt)
    chunk_decay = jnp.exp(a_cs[:, :, -1])

    def step(h, inp):
        s_c, d_c = inp
        return h * d_c[..., None, None] + s_c, h

    h0 = jnp.zeros((b, g, r, p, n), f32)
    _, prev = lax.scan(step, h0, (jnp.moveaxis(states, 1, 0), jnp.moveaxis(chunk_decay, 1, 0)))
    prev = jnp.moveaxis(prev, 0, 1)
    y_off = jnp.einsum("bclgn,bcgrpn,bclgr->bclgrp", Cc, prev, jnp.exp(a_cs))
    return (y_diag + y_off).reshape(b, T, g, r, p)


def ssd_mixer(u, in_proj, conv_w, conv_b, dt_bias, a_log, d_skip, norm_w, out_proj):
    b, T, _ = u.shape
    f32 = jnp.float32
    zxbcdt = u @ in_proj
    z = zxbcdt[..., :D_INNER]
    xbc = zxbcdt[..., D_INNER:D_INNER + D_XBC]
    dt_raw = zxbcdt[..., D_INNER + D_XBC:]
    xbc = jax.nn.silu(centred_depthwise_conv(xbc, conv_w, conv_b))
    xs = xbc[..., :D_INNER].reshape(b, T, N_GROUPS, HEADS_PER_GROUP, HEAD_DIM)
    Bm = xbc[..., D_INNER:D_INNER + D_BC].reshape(b, T, N_GROUPS, D_STATE)
    Cm = xbc[..., D_INNER + D_BC:].reshape(b, T, N_GROUPS, D_STATE)
    dt = jax.nn.softplus(dt_raw.astype(f32).reshape(b, T, 2, N_GROUPS, HEADS_PER_GROUP)
                         + dt_bias.astype(f32).reshape(2, N_GROUPS, HEADS_PER_GROUP))
    A = -jnp.exp(a_log.astype(f32)).reshape(2, N_GROUPS, HEADS_PER_GROUP)
    flip = lambda t: jnp.flip(t, axis=1)
    y_fwd = ssd_chunked(xs, dt[:, :, 0], A[0], Bm, Cm)
    y_bwd = flip(ssd_chunked(flip(xs), flip(dt[:, :, 1]), A[1], flip(Bm), flip(Cm)))
    y = y_fwd + y_bwd + xs.astype(f32) * d_skip.astype(f32).reshape(N_GROUPS, HEADS_PER_GROUP)[..., None]
    gy = (y.reshape(b, T, D_INNER) * jax.nn.silu(z.astype(f32))).reshape(b, T, N_GROUPS, D_INNER // N_GROUPS)
    gy = gy * lax.rsqrt(jnp.mean(jnp.square(gy), axis=-1, keepdims=True) + RMS_EPS)
    gy = gy.reshape(b, T, D_INNER) * norm_w.astype(f32)
    return gy.astype(u.dtype) @ out_proj


def pool_mixer(u, w, bias, scale):
    b, T, _ = u.shape
    uf = u.astype(jnp.float32)
    cs = jnp.concatenate([jnp.zeros((b, 1, D_MODEL), jnp.float32), jnp.cumsum(uf, axis=1)], axis=1)
    t = jnp.arange(T)
    groups = []
    for gi, win in enumerate(POOL_WINDOWS):
        lo_c, hi_c = gi * POOL_GROUP_DIM, (gi + 1) * POOL_GROUP_DIM
        start = t - win // 2
        lo = jnp.clip(start, 0, T)
        hi = jnp.clip(start + win, 0, T)
        csg = cs[..., lo_c:hi_c]
        wsum = jnp.take(csg, hi, axis=1) - jnp.take(csg, lo, axis=1)
        cnt = (hi - lo).astype(jnp.float32)[None, :, None]
        groups.append(wsum / cnt - uf[..., lo_c:hi_c])
    m = jnp.stack(groups, axis=2).astype(u.dtype)
    y = jnp.einsum("btgc,gcd->btgd", m, w) + bias
    return y.reshape(b, T, D_MODEL) * scale


def sq_relu_mlp(u, w1, w2):
    return jnp.square(jax.nn.relu(u @ w1)) @ w2


def setup_inputs(seed: int = 0) -> dict:
    key = jax.random.key(seed)
    ks = jax.random.split(key, 20)
    f32 = jnp.float32
    nrm = lambda k, shape: jax.random.normal(k, shape, f32)
    x = nrm(ks[0], (BATCH, SEQ, D_MODEL))
    ssd_in_proj = nrm(ks[1], (N_SSD_LAYERS, D_MODEL, D_IN_PROJ)) * D_MODEL ** -0.5
    ssd_conv_w = nrm(ks[2], (N_SSD_LAYERS, CONV_WIDTH, 1, D_XBC)) * CONV_WIDTH ** -0.5
    ssd_conv_b = 0.01 * nrm(ks[3], (N_SSD_LAYERS, D_XBC))
    u_dt = jax.random.uniform(ks[4], (N_SSD_LAYERS, 2, N_HEADS), f32)
    dt0 = jnp.exp(u_dt * (math.log(DT_MAX) - math.log(DT_MIN)) + math.log(DT_MIN))
    dt0 = jnp.maximum(dt0, 1e-4)
    ssd_dt_bias = dt0 + jnp.log(-jnp.expm1(-dt0))
    ssd_A_log = jnp.log(jax.random.uniform(ks[5], (N_SSD_LAYERS, 2, N_HEADS), f32, A_INIT_MIN, A_INIT_MAX))
    ssd_D = 1.0 + 0.1 * nrm(ks[6], (N_SSD_LAYERS, N_HEADS))
    ssd_norm_w = 1.0 + 0.1 * nrm(ks[7], (N_SSD_LAYERS, D_INNER))
    ssd_out_proj = nrm(ks[8], (N_SSD_LAYERS, D_INNER, D_MODEL)) * (D_INNER ** -0.5 * DEEPNORM_BETA)
    pool_w = nrm(ks[9], (N_POOL_LAYERS, N_POOL_GROUPS, POOL_GROUP_DIM, POOL_GROUP_DIM)) * (POOL_GROUP_DIM ** -0.5 * DEEPNORM_BETA)
    pool_b = 0.01 * nrm(ks[10], (N_POOL_LAYERS, N_POOL_GROUPS, POOL_GROUP_DIM))
    pool_scale = 1.0 + 0.1 * nrm(ks[11], (N_POOL_LAYERS, D_MODEL))
    mlp_w1 = nrm(ks[12], (DEPTH, D_MODEL, D_FF)) * D_MODEL ** -0.5
    mlp_w2 = nrm(ks[13], (DEPTH, D_FF, D_MODEL)) * (D_FF ** -0.5 * DEEPNORM_BETA)
    ln_mix_g = 1.0 + 0.1 * nrm(ks[14], (DEPTH, D_MODEL))
    ln_mix_b = 0.01 * nrm(ks[15], (DEPTH, D_MODEL))
    ln_ffn_g = 1.0 + 0.1 * nrm(ks[16], (DEPTH, D_MODEL))
    ln_ffn_b = 0.01 * nrm(ks[17], (DEPTH, D_MODEL))
    return {"x": x, "ssd_in_proj": ssd_in_proj, "ssd_conv_w": ssd_conv_w, "ssd_conv_b": ssd_conv_b,
            "ssd_dt_bias": ssd_dt_bias, "ssd_A_log": ssd_A_log, "ssd_D": ssd_D, "ssd_norm_w": ssd_norm_w,
            "ssd_out_proj": ssd_out_proj, "pool_w": pool_w, "pool_b": pool_b, "pool_scale": pool_scale,
            "mlp_w1": mlp_w1, "mlp_w2": mlp_w2, "ln_mix_g": ln_mix_g, "ln_mix_b": ln_mix_b,
            "ln_ffn_g": ln_ffn_g, "ln_ffn_b": ln_ffn_b}


def reference(x, ssd_in_proj, ssd_conv_w, ssd_conv_b, ssd_dt_bias, ssd_A_log, ssd_D, ssd_norm_w,
              ssd_out_proj, pool_w, pool_b, pool_scale, mlp_w1, mlp_w2, ln_mix_g, ln_mix_b,
              ln_ffn_g, ln_ffn_b):
    for i in range(DEPTH):
        j = i // N_MIXERS
        if i % N_MIXERS == 0:
            mix = ssd_mixer(x, ssd_in_proj[j], ssd_conv_w[j], ssd_conv_b[j], ssd_dt_bias[j],
                            ssd_A_log[j], ssd_D[j], ssd_norm_w[j], ssd_out_proj[j])
        else:
            mix = pool_mixer(x, pool_w[j], pool_b[j], pool_scale[j])
        x = layer_norm(DEEPNORM_ALPHA * x + mix, ln_mix_g[i], ln_mix_b[i])
        x = layer_norm(DEEPNORM_ALPHA * x + sq_relu_mlp(x, mlp_w1[i], mlp_w2[i]), ln_ffn_g[i], ln_ffn_b[i])
    return x
```

```python
import functools

import jax
import jax.numpy as jnp
from jax import lax
from jax.experimental import pallas as pl
from jax.experimental.pallas import tpu as pltpu

F32 = jnp.float32
BF16 = jnp.bfloat16

D_MODEL = 2048
DEPTH = 4
D_INNER = 2 * D_MODEL
HEAD_DIM = 64
N_HEADS = D_INNER // HEAD_DIM
N_GROUPS = 8
HEADS_PER_GROUP = N_HEADS // N_GROUPS
GROUP_DIM = D_INNER // N_GROUPS
D_STATE = 128
CONV_WIDTH = 5
CHUNK = 128
D_BC = N_GROUPS * D_STATE
D_XBC = D_INNER + 2 * D_BC
D_IN_PROJ = D_INNER + D_XBC + 2 * N_HEADS
POOL_WINDOWS = (2, 4, 8, 16)
POOL_GROUP_DIM = D_MODEL // len(POOL_WINDOWS)
D_FF = 4 * D_MODEL
DEEPNORM_ALPHA = (2.0 * DEPTH) ** 0.25
LN_EPS = 1e-5
RMS_EPS = 1e-5

SUBLANES = 8
LANES = 128
VMEM_LIMIT = 56 * 1024 * 1024

MLP_BM = 512
MLP_BF = 512
INPROJ_BM = 1024
INPROJ_BN = 1152
OUTPROJ_BM = 512
OUTPROJ_BK = 512
CONV_BT = 512
CONV_BC = 512
POOL_BT = 512
SSD_CPS = 4
HALO = SUBLANES


def _layer_norm(v, g, b):
    mu = jnp.mean(v, axis=-1, keepdims=True)
    d = v - mu
    var = jnp.mean(d * d, axis=-1, keepdims=True)
    return d * lax.rsqrt(var + LN_EPS) * g + b


def _silu(v):
    return v * (1.0 / (1.0 + jnp.exp(-v)))


def _split_bf16(v, parts):
    out = []
    r = v
    for _ in range(parts):
        p = r.astype(BF16)
        out.append(p)
        r = r - p.astype(F32)
    return out


def _matmul_kernel(x_ref, w_ref, o_ref):
    o_ref[...] = jnp.dot(x_ref[...], w_ref[...], preferred_element_type=F32)


def _in_proj(xb, w):
    m, k = xb.shape
    n = w.shape[1]
    return pl.pallas_call(
        _matmul_kernel,
        out_shape=jax.ShapeDtypeStruct((m, n), F32),
        grid=(m // INPROJ_BM, n // INPROJ_BN),
        in_specs=[pl.BlockSpec((INPROJ_BM, k), lambda i, j: (i, 0)),
                  pl.BlockSpec((k, INPROJ_BN), lambda i, j: (0, j))],
        out_specs=pl.BlockSpec((INPROJ_BM, INPROJ_BN), lambda i, j: (i, j)),
        compiler_params=pltpu.CompilerParams(
            dimension_semantics=("parallel", "parallel"), vmem_limit_bytes=VMEM_LIMIT),
        name="in_proj",
    )(xb, w)


def _conv_kernel(seq_tiles, m_ref, p_ref, n_ref, w_ref, b_ref, o_ref, ext_ref):
    bt = m_ref.shape[0]
    pos = pl.program_id(0) % seq_tiles
    ext_ref[0:HALO, :] = jnp.where(pos == 0, 0.0, p_ref[...])
    ext_ref[HALO:HALO + bt, :] = m_ref[...]
    ext_ref[HALO + bt:2 * HALO + bt, :] = jnp.where(pos == seq_tiles - 1, 0.0, n_ref[...])
    pad = CONV_WIDTH // 2
    acc = b_ref[...] + w_ref[0:1, :] * ext_ref[pl.ds(HALO - pad, bt), :]
    for k in range(1, CONV_WIDTH):
        acc = acc + w_ref[k:k + 1, :] * ext_ref[pl.ds(HALO - pad + k, bt), :]
    o_ref[...] = _silu(acc)


def _conv_silu(zxbcdt, conv_w, conv_b, seq):
    m = zxbcdt.shape[0]
    col0 = D_INNER // CONV_BC
    rb = CONV_BT // HALO
    n_halo_blocks = m // HALO
    return pl.pallas_call(
        functools.partial(_conv_kernel, seq // CONV_BT),
        out_shape=jax.ShapeDtypeStruct((m, D_XBC), F32),
        grid=(m // CONV_BT, D_XBC // CONV_BC),
        in_specs=[
            pl.BlockSpec((CONV_BT, CONV_BC), lambda i, j: (i, j + col0)),
            pl.BlockSpec((HALO, CONV_BC), lambda i, j: (jnp.maximum(i * rb - 1, 0), j + col0)),
            pl.BlockSpec((HALO, CONV_BC),
                         lambda i, j: (jnp.minimum((i + 1) * rb, n_halo_blocks - 1), j + col0)),
            pl.BlockSpec((CONV_WIDTH, CONV_BC), lambda i, j: (0, j)),
            pl.BlockSpec((1, CONV_BC), lambda i, j: (0, j)),
        ],
        out_specs=pl.BlockSpec((CONV_BT, CONV_BC), lambda i, j: (i, j)),
        scratch_shapes=[pltpu.VMEM((CONV_BT + 2 * HALO, CONV_BC), F32)],
        compiler_params=pltpu.CompilerParams(
            dimension_semantics=("parallel", "parallel"), vmem_limit_bytes=VMEM_LIMIT),
        name="conv_silu",
    )(zxbcdt, zxbcdt, zxbcdt, conv_w, conv_b)


ROW_CS, ROW_DT, ROW_WEND, ROW_WIN, ROW_CDEC = (HEADS_PER_GROUP * r for r in range(5))


def _ssd_kernel(x_ref, b_ref, c_ref, z_ref, dt_ref, bias_ref, alog_ref, dskip_ref, nw_ref,
                tri_ref, negmask_ref, ecs_ref, ef_ref, o_ref, h_ref, yf_ref, yd_ref):
    ph = pl.program_id(2)
    cb = pl.program_id(3)
    n_cb = pl.num_programs(3)
    hpg, hd = HEADS_PER_GROUP, HEAD_DIM

    @pl.when(cb == 0)
    def _():
        h_ref[...] = jnp.zeros_like(h_ref)

    a_head = -jnp.exp(alog_ref[0, 0])
    bias = bias_ref[0, 0]
    tri = tri_ref[0]
    negmask = negmask_ref[0]
    blk = cb + ph * (n_cb - 1 - 2 * cb)

    for k in range(SSD_CPS):
        kk = k + ph * (SSD_CPS - 1 - 2 * k)
        r0 = pl.multiple_of(kk * CHUNK, CHUNK)
        rows = pl.ds(r0, CHUNK)
        x = x_ref[rows, :]
        bmat = b_ref[rows, :].astype(BF16)
        cmat = c_ref[rows, :].astype(BF16)

        raw = dt_ref[0, 0, 0, :, pl.ds(r0, CHUNK)] + bias
        dt = jnp.maximum(raw, 0.0) + jnp.log1p(jnp.exp(-jnp.abs(raw)))
        a = dt * a_head
        parts = jnp.concatenate(
            [p.astype(F32) for p in _split_bf16(a, 3)] + [jnp.zeros_like(a)], axis=0).astype(BF16)
        cs3 = jnp.dot(parts, tri, preferred_element_type=F32)
        cs = cs3[0:hpg] + cs3[hpg:2 * hpg] + cs3[2 * hpg:3 * hpg]
        tot = jnp.where(ph == 0, cs[:, CHUNK - 1:CHUNK], cs[:, 0:1])
        w_end = jnp.exp(tot - cs) * dt
        w_in = jnp.exp(cs)
        cdec = jnp.broadcast_to(jnp.exp(tot), (hpg, CHUNK))
        table = jnp.concatenate(
            [cs, dt, w_end, w_in, cdec, jnp.zeros((CHUNK - 5 * hpg, CHUNK), F32)], axis=0)
        tab_t = table.T
        t_hi, t_mid, t_lo = _split_bf16(tab_t, 3)
        ecs = ecs_ref[...]
        ef = ef_ref[...]
        cs_b = (jnp.dot(t_hi, ecs, preferred_element_type=F32)
                + jnp.dot(t_mid, ecs, preferred_element_type=F32)
                + jnp.dot(t_lo, ecs, preferred_element_type=F32))
        fe = (jnp.dot(t_hi, ef, preferred_element_type=F32)
              + jnp.dot(t_mid, ef, preferred_element_type=F32))
        dt_e = fe[:, 0:GROUP_DIM]
        wend_e = fe[:, GROUP_DIM:2 * GROUP_DIM]
        win_e = fe[:, 2 * GROUP_DIM:3 * GROUP_DIM]
        cdec_e = fe[:, 3 * GROUP_DIM:4 * GROUP_DIM]

        x1 = (x * dt_e).astype(BF16)
        x2 = (x * wend_e).astype(BF16)
        scores = lax.dot_general(cmat, bmat, (((1,), (1,)), ((), ())),
                                 preferred_element_type=F32)
        for h in range(hpg):
            seg = cs_b[:, h * CHUNK:(h + 1) * CHUNK] - cs[h:h + 1, :]
            m_h = (scores * jnp.exp(seg + negmask)).astype(BF16)
            yd_ref[:, h * hd:(h + 1) * hd] = jnp.dot(
                m_h, x1[:, h * hd:(h + 1) * hd], preferred_element_type=F32)
        h_prev = h_ref[...]
        y = yd_ref[...] + jnp.dot(cmat, h_prev.astype(BF16), preferred_element_type=F32) * win_e
        st = lax.dot_general(bmat, x2, (((0,), (0,)), ((), ())), preferred_element_type=F32)
        h_ref[...] = h_prev * cdec_e + st

        g0 = pl.multiple_of((blk * SSD_CPS + kk) * CHUNK, CHUNK)

        @pl.when(ph == 0)
        def _():
            yf_ref[pl.ds(g0, CHUNK), :] = y

        @pl.when(ph == 1)
        def _():
            ytot = yf_ref[pl.ds(g0, CHUNK), :] + y + x * dskip_ref[...]
            gy = ytot * _silu(z_ref[rows, :])
            ms = jnp.mean(gy * gy, axis=-1, keepdims=True)
            o_ref[rows, :] = (gy * lax.rsqrt(ms + RMS_EPS) * nw_ref[...]).astype(o_ref.dtype)


def _ssd_constants():
    r = lax.broadcasted_iota(jnp.int32, (CHUNK, CHUNK), 0)
    c = lax.broadcasted_iota(jnp.int32, (CHUNK, CHUNK), 1)
    feeds = jnp.stack([c <= r, c >= r])
    negmask = jnp.where(feeds, 0.0, -jnp.inf).astype(F32)
    tri = jnp.swapaxes(feeds, 1, 2).astype(BF16)
    j = lax.broadcasted_iota(jnp.int32, (CHUNK, HEADS_PER_GROUP * CHUNK), 0)
    n = lax.broadcasted_iota(jnp.int32, (CHUNK, HEADS_PER_GROUP * CHUNK), 1)
    ecs = (j == ROW_CS + n // CHUNK).astype(BF16)
    j = lax.broadcasted_iota(jnp.int32, (CHUNK, 4 * GROUP_DIM), 0)
    n = lax.broadcasted_iota(jnp.int32, (CHUNK, 4 * GROUP_DIM), 1)
    ef = (j == ROW_DT + HEADS_PER_GROUP * (n // GROUP_DIM) + (n % GROUP_DIM) // HEAD_DIM).astype(BF16)
    return tri, negmask, ecs, ef


def _ssd_scan(xbc, zxbcdt, dt_t, dt_bias, a_log, d_skip, norm_w, batch, seq):
    m = xbc.shape[0]
    cl = SSD_CPS * CHUNK
    n_cb = seq // cl
    tri, negmask, ecs, ef = _ssd_constants()

    def in_blk(b, ph, c):
        return b * n_cb + c + ph * (n_cb - 1 - 2 * c)

    def out_blk(b, ph, c):
        return b * n_cb + n_cb - 1 - ph * c

    bcol = D_INNER // D_STATE
    ccol = (D_INNER + D_BC) // D_STATE
    const2 = lambda b, g, ph, c: (0, 0)
    return pl.pallas_call(
        _ssd_kernel,
        out_shape=jax.ShapeDtypeStruct((m, D_INNER), BF16),
        grid=(batch, N_GROUPS, 2, n_cb),
        in_specs=[
            pl.BlockSpec((cl, GROUP_DIM), lambda b, g, ph, c: (in_blk(b, ph, c), g)),
            pl.BlockSpec((cl, D_STATE), lambda b, g, ph, c: (in_blk(b, ph, c), bcol + g)),
            pl.BlockSpec((cl, D_STATE), lambda b, g, ph, c: (in_blk(b, ph, c), ccol + g)),
            pl.BlockSpec((cl, GROUP_DIM), lambda b, g, ph, c: (out_blk(b, ph, c), g)),
            pl.BlockSpec((1, 1, 1, HEADS_PER_GROUP, cl),
                         lambda b, g, ph, c: (b, ph, g, 0, c + ph * (n_cb - 1 - 2 * c))),
            pl.BlockSpec((1, 1, HEADS_PER_GROUP, 1), lambda b, g, ph, c: (ph, g, 0, 0)),
            pl.BlockSpec((1, 1, HEADS_PER_GROUP, 1), lambda b, g, ph, c: (ph, g, 0, 0)),
            pl.BlockSpec((1, GROUP_DIM), lambda b, g, ph, c: (0, g)),
            pl.BlockSpec((1, GROUP_DIM), lambda b, g, ph, c: (0, g)),
            pl.BlockSpec((1, CHUNK, CHUNK), lambda b, g, ph, c: (ph, 0, 0)),
            pl.BlockSpec((1, CHUNK, CHUNK), lambda b, g, ph, c: (ph, 0, 0)),
            pl.BlockSpec(ecs.shape, const2),
            pl.BlockSpec(ef.shape, const2),
        ],
        out_specs=pl.BlockSpec((cl, GROUP_DIM), lambda b, g, ph, c: (out_blk(b, ph, c), g)),
        scratch_shapes=[pltpu.VMEM((D_STATE, GROUP_DIM), F32),
                        pltpu.VMEM((seq, GROUP_DIM), F32),
                        pltpu.VMEM((CHUNK, GROUP_DIM), F32)],
        compiler_params=pltpu.CompilerParams(
            dimension_semantics=("arbitrary", "arbitrary", "arbitrary", "arbitrary"),
            vmem_limit_bytes=VMEM_LIMIT),
        name="ssd_scan",
    )(xbc, xbc, xbc, zxbcdt, dt_t, dt_bias, a_log, d_skip, norm_w, tri, negmask, ecs, ef)


def _store_norm(v, g_ref, b_ref, o_ref, ob_ref):
    y = _layer_norm(v, g_ref[...], b_ref[...])
    o_ref[...] = y
    if ob_ref is not None:
        ob_ref[...] = y.astype(BF16)


def _outproj_kernel(a_ref, w_ref, x_ref, g_ref, b_ref, o_ref, ob_ref, acc_ref):
    j = pl.program_id(1)
    p = jnp.dot(a_ref[...], w_ref[...], preferred_element_type=F32)

    @pl.when(j == 0)
    def _():
        acc_ref[...] = p

    @pl.when(j > 0)
    def _():
        acc_ref[...] += p

    @pl.when(j == pl.num_programs(1) - 1)
    def _():
        _store_norm(DEEPNORM_ALPHA * x_ref[...] + acc_ref[...], g_ref, b_ref, o_ref, ob_ref)


def _out_proj_norm(gy, w, x, g, b):
    m, k = gy.shape
    n = w.shape[1]
    row = lambda i, j: (i, 0)
    vec = lambda i, j: (0, 0)
    return pl.pallas_call(
        _outproj_kernel,
        out_shape=(jax.ShapeDtypeStruct((m, n), F32), jax.ShapeDtypeStruct((m, n), BF16)),
        grid=(m // OUTPROJ_BM, k // OUTPROJ_BK),
        in_specs=[pl.BlockSpec((OUTPROJ_BM, OUTPROJ_BK), lambda i, j: (i, j)),
                  pl.BlockSpec((OUTPROJ_BK, n), lambda i, j: (j, 0)),
                  pl.BlockSpec((OUTPROJ_BM, n), row),
                  pl.BlockSpec((1, n), vec), pl.BlockSpec((1, n), vec)],
        out_specs=(pl.BlockSpec((OUTPROJ_BM, n), row), pl.BlockSpec((OUTPROJ_BM, n), row)),
        scratch_shapes=[pltpu.VMEM((OUTPROJ_BM, n), F32)],
        compiler_params=pltpu.CompilerParams(
            dimension_semantics=("parallel", "arbitrary"), vmem_limit_bytes=VMEM_LIMIT),
        name="out_proj_norm",
    )(gy, w, x, g, b)


def _pool_kernel(seq, x_ref, p_ref, n_ref, w_ref, bias_ref, scale_ref, g_ref, b_ref,
                 o_ref, ob_ref, ext_ref, v_ref):
    bt = x_ref.shape[0]
    seq_tiles = seq // bt
    pos = pl.program_id(0) % seq_tiles
    ext_ref[0:HALO, :] = jnp.where(pos == 0, 0.0, p_ref[...])
    ext_ref[HALO:HALO + bt, :] = x_ref[...]
    ext_ref[HALO + bt:2 * HALO + bt, :] = jnp.where(pos == seq_tiles - 1, 0.0, n_ref[...])
    t = pos * bt + lax.broadcasted_iota(jnp.int32, (bt, 1), 0)
    gd = POOL_GROUP_DIM
    for gi, win in enumerate(POOL_WINDOWS):
        cols = slice(gi * gd, (gi + 1) * gd)
        start = t - win // 2
        cnt = (jnp.clip(start + win, 0, seq) - jnp.clip(start, 0, seq)).astype(F32)
        wsum = ext_ref[pl.ds(HALO - win // 2, bt), cols]
        for k in range(1, win):
            wsum = wsum + ext_ref[pl.ds(HALO - win // 2 + k, bt), cols]
        xg = x_ref[:, cols]
        mg = (wsum / cnt - xg).astype(BF16)
        y = jnp.dot(mg, w_ref[gi], preferred_element_type=F32) + bias_ref[:, cols]
        v_ref[:, cols] = DEEPNORM_ALPHA * xg + y * scale_ref[:, cols]
    _store_norm(v_ref[...], g_ref, b_ref, o_ref, ob_ref)


def _pool_mixer_norm(x, w, bias, scale, g, b, seq):
    m, d = x.shape
    rb = POOL_BT // HALO
    n_halo_blocks = m // HALO
    row = lambda i: (i, 0)
    vec = lambda i: (0, 0)
    return pl.pallas_call(
        functools.partial(_pool_kernel, seq),
        out_shape=(jax.ShapeDtypeStruct((m, d), F32), jax.ShapeDtypeStruct((m, d), BF16)),
        grid=(m // POOL_BT,),
        in_specs=[pl.BlockSpec((POOL_BT, d), row),
                  pl.BlockSpec((HALO, d), lambda i: (jnp.maximum(i * rb - 1, 0), 0)),
                  pl.BlockSpec((HALO, d), lambda i: (jnp.minimum((i + 1) * rb, n_halo_blocks - 1), 0)),
                  pl.BlockSpec(w.shape, lambda i: (0, 0, 0)),
                  pl.BlockSpec((1, d), vec), pl.BlockSpec((1, d), vec),
                  pl.BlockSpec((1, d), vec), pl.BlockSpec((1, d), vec)],
        out_specs=(pl.BlockSpec((POOL_BT, d), row), pl.BlockSpec((POOL_BT, d), row)),
        scratch_shapes=[pltpu.VMEM((POOL_BT + 2 * HALO, d), F32), pltpu.VMEM((POOL_BT, d), F32)],
        compiler_params=pltpu.CompilerParams(
            dimension_semantics=("parallel",), vmem_limit_bytes=VMEM_LIMIT),
        name="pool_mixer_norm",
    )(x, x, x, w, bias, scale, g, b)


def _mlp_kernel(emit_bf16, x_ref, xb_ref, w1_ref, w2_ref, g_ref, b_ref, o_ref, *rest):
    ob_ref, acc_ref = rest if emit_bf16 else (None, rest[0])
    j = pl.program_id(1)
    hid = jnp.dot(xb_ref[...], w1_ref[...], preferred_element_type=F32)
    hid = jnp.square(jnp.maximum(hid, 0.0)).astype(BF16)
    p = jnp.dot(hid, w2_ref[...], preferred_element_type=F32)

    @pl.when(j == 0)
    def _():
        acc_ref[...] = p

    @pl.when(j > 0)
    def _():
        acc_ref[...] += p

    @pl.when(j == pl.num_programs(1) - 1)
    def _():
        _store_norm(DEEPNORM_ALPHA * x_ref[...] + acc_ref[...], g_ref, b_ref, o_ref, ob_ref)


def _mlp_norm(x, xb, w1, w2, g, b, emit_bf16):
    m, d = x.shape
    f = w1.shape[1]
    row = lambda i, j: (i, 0)
    vec = lambda i, j: (0, 0)
    out_shape = [jax.ShapeDtypeStruct((m, d), F32)]
    out_specs = [pl.BlockSpec((MLP_BM, d), row)]
    if emit_bf16:
        out_shape.append(jax.ShapeDtypeStruct((m, d), BF16))
        out_specs.append(pl.BlockSpec((MLP_BM, d), row))
    out = pl.pallas_call(
        functools.partial(_mlp_kernel, emit_bf16),
        out_shape=tuple(out_shape),
        grid=(m // MLP_BM, f // MLP_BF),
        in_specs=[pl.BlockSpec((MLP_BM, d), row), pl.BlockSpec((MLP_BM, d), row),
                  pl.BlockSpec((d, MLP_BF), lambda i, j: (0, j)),
                  pl.BlockSpec((MLP_BF, d), lambda i, j: (j, 0)),
                  pl.BlockSpec((1, d), vec), pl.BlockSpec((1, d), vec)],
        out_specs=tuple(out_specs),
        scratch_shapes=[pltpu.VMEM((MLP_BM, d), F32)],
        compiler_params=pltpu.CompilerParams(
            dimension_semantics=("parallel", "arbitrary"), vmem_limit_bytes=VMEM_LIMIT),
        name="mlp_norm",
    )(x, xb, w1, w2, g, b)
    return out if emit_bf16 else (out[0], None)


def kernel(x, ssd_in_proj, ssd_conv_w, ssd_conv_b, ssd_dt_bias, ssd_A_log, ssd_D, ssd_norm_w,
           ssd_out_proj, pool_w, pool_b, pool_scale, mlp_w1, mlp_w2, ln_mix_g, ln_mix_b,
           ln_ffn_g, ln_ffn_b):
    batch, seq, d = x.shape
    m = batch * seq
    assert d == D_MODEL and seq % (SSD_CPS * CHUNK) == 0 and seq % POOL_BT == 0 and seq % CONV_BT == 0
    xf = x.reshape(m, d)
    xb = xf.astype(BF16)
    vec = lambda v: v.reshape(1, -1)
    for i in range(DEPTH):
        j = i // 2
        if i % 2 == 0:
            zxbcdt = _in_proj(xb, ssd_in_proj[j].astype(BF16))
            xbc = _conv_silu(zxbcdt, ssd_conv_w[j].reshape(CONV_WIDTH, D_XBC), vec(ssd_conv_b[j]), seq)
            dt_t = zxbcdt[:, D_INNER + D_XBC:].reshape(batch, seq, 2, N_GROUPS, HEADS_PER_GROUP)
            dt_t = jnp.transpose(dt_t, (0, 2, 3, 4, 1))
            head_shape = (2, N_GROUPS, HEADS_PER_GROUP, 1)
            gy = _ssd_scan(xbc, zxbcdt, dt_t, ssd_dt_bias[j].reshape(head_shape),
                           ssd_A_log[j].reshape(head_shape), vec(jnp.repeat(ssd_D[j], HEAD_DIM)),
                           vec(ssd_norm_w[j]), batch, seq)
            xf, xb = _out_proj_norm(gy, ssd_out_proj[j].astype(BF16), xf,
                                    vec(ln_mix_g[i]), vec(ln_mix_b[i]))
        else:
            xf, xb = _pool_mixer_norm(xf, pool_w[j].astype(BF16), vec(pool_b[j]), vec(pool_scale[j]),
                                      vec(ln_mix_g[i]), vec(ln_mix_b[i]), seq)
        xf, xb = _mlp_norm(xf, xb, mlp_w1[i].astype(BF16), mlp_w2[i].astype(BF16),
                           vec(ln_ffn_g[i]), vec(ln_ffn_b[i]), emit_bf16=i + 1 < DEPTH)
    return xf.reshape(batch, seq, d)
```

```python
import functools

import jax
import jax.numpy as jnp
from jax import lax
from jax.experimental import pallas as pl
from jax.experimental.pallas import tpu as pltpu

F32 = jnp.float32
BF16 = jnp.bfloat16

D_MODEL = 2048
DEPTH = 4
D_INNER = 2 * D_MODEL
HEAD_DIM = 64
N_HEADS = D_INNER // HEAD_DIM
N_GROUPS = 8
HEADS_PER_GROUP = N_HEADS // N_GROUPS
GROUP_DIM = D_INNER // N_GROUPS
D_STATE = 128
CONV_WIDTH = 5
CHUNK = 128
D_BC = N_GROUPS * D_STATE
D_XBC = D_INNER + 2 * D_BC
D_IN_PROJ = D_INNER + D_XBC + 2 * N_HEADS
POOL_WINDOWS = (2, 4, 8, 16)
POOL_GROUP_DIM = D_MODEL // len(POOL_WINDOWS)
D_FF = 4 * D_MODEL
DEEPNORM_ALPHA = (2.0 * DEPTH) ** 0.25
LN_EPS = 1e-5
RMS_EPS = 1e-5

SUBLANES = 8
LANES = 128
VMEM_LIMIT = 56 * 1024 * 1024

MLP_BM = 1024
MLP_BF = 512
INPROJ_BM = 1024
INPROJ_BN = 1152
OUTPROJ_BM = 1024
OUTPROJ_BK = 1024
CONV_BT = 1024
CONV_BC = 512
POOL_BT = 512
SSD_CPS = 4
HALO = SUBLANES
SINGLE_BUFFER = pl.Buffered(1)


def _layer_norm(v, g, b):
    mu = jnp.mean(v, axis=-1, keepdims=True)
    d = v - mu
    var = jnp.mean(d * d, axis=-1, keepdims=True)
    return d * lax.rsqrt(var + LN_EPS) * g + b


def _silu(v):
    return v * (1.0 / (1.0 + jnp.exp(-v)))


def _split_bf16(v, parts):
    out = []
    r = v
    for _ in range(parts):
        p = r.astype(BF16)
        out.append(p)
        r = r - p.astype(F32)
    return out


def _matmul_kernel(x_ref, w_ref, o_ref):
    o_ref[...] = jnp.dot(x_ref[...], w_ref[...].astype(BF16), preferred_element_type=F32)


def _in_proj(xb, w_all, layer):
    m, k = xb.shape
    n = w_all.shape[2]
    return pl.pallas_call(
        _matmul_kernel,
        out_shape=jax.ShapeDtypeStruct((m, n), F32),
        grid=(m // INPROJ_BM, n // INPROJ_BN),
        in_specs=[pl.BlockSpec((INPROJ_BM, k), lambda i, j: (i, 0)),
                  pl.BlockSpec((None, k, INPROJ_BN), lambda i, j: (layer, 0, j))],
        out_specs=pl.BlockSpec((INPROJ_BM, INPROJ_BN), lambda i, j: (i, j)),
        compiler_params=pltpu.CompilerParams(
            dimension_semantics=("parallel", "parallel"), vmem_limit_bytes=VMEM_LIMIT),
        name="in_proj",
    )(xb, w_all)


def _conv_kernel(seq_tiles, m_ref, p_ref, n_ref, w_ref, b_ref, o_ref):
    bt = m_ref.shape[0]
    pos = pl.program_id(0) % seq_tiles
    ext = jnp.concatenate([jnp.where(pos == 0, 0.0, p_ref[...]), m_ref[...],
                           jnp.where(pos == seq_tiles - 1, 0.0, n_ref[...])], axis=0)
    pad = CONV_WIDTH // 2
    acc = b_ref[...] + w_ref[pad:pad + 1, :] * m_ref[...]
    for k in range(CONV_WIDTH):
        if k != pad:
            shifted = pltpu.roll(ext, (pad - k) % ext.shape[0], axis=0)[HALO:HALO + bt]
            acc = acc + w_ref[k:k + 1, :] * shifted
    o_ref[...] = _silu(acc).astype(o_ref.dtype)


def _conv_silu(zxbcdt, conv_w_all, conv_b_all, layer, seq, col_start, n_cols, out_dtype):
    m = zxbcdt.shape[0]
    in0 = (D_INNER + col_start) // CONV_BC
    w0 = col_start // CONV_BC
    rb = CONV_BT // HALO
    n_halo_blocks = m // HALO
    return pl.pallas_call(
        functools.partial(_conv_kernel, seq // CONV_BT),
        out_shape=jax.ShapeDtypeStruct((m, n_cols), out_dtype),
        grid=(m // CONV_BT, n_cols // CONV_BC),
        in_specs=[
            pl.BlockSpec((CONV_BT, CONV_BC), lambda i, j: (i, j + in0)),
            pl.BlockSpec((HALO, CONV_BC), lambda i, j: (jnp.maximum(i * rb - 1, 0), j + in0)),
            pl.BlockSpec((HALO, CONV_BC),
                         lambda i, j: (jnp.minimum((i + 1) * rb, n_halo_blocks - 1), j + in0)),
            pl.BlockSpec((None, CONV_WIDTH, CONV_BC), lambda i, j: (layer, 0, j + w0)),
            pl.BlockSpec((None, 1, CONV_BC), lambda i, j: (layer, 0, j + w0)),
        ],
        out_specs=pl.BlockSpec((CONV_BT, CONV_BC), lambda i, j: (i, j)),
        compiler_params=pltpu.CompilerParams(
            dimension_semantics=("parallel", "parallel"), vmem_limit_bytes=VMEM_LIMIT),
        name="conv_silu",
    )(zxbcdt, zxbcdt, zxbcdt, conv_w_all, conv_b_all)


ROW_CS, ROW_WEND, ROW_WIN = (HEADS_PER_GROUP * r for r in range(3))
TABLE_ROWS = 3 * HEADS_PER_GROUP


def _ssd_block(rev, x_ref, b_ref, c_ref, z_ref, dt_ref, bias_ref, alog_ref, dskip_ref, nw_ref,
               tri_ref, negmask_ref, ef_ref, o_ref, h_ref, yf_ref):
    hpg = HEADS_PER_GROUP
    cl = SSD_CPS * CHUNK
    cb = pl.program_id(3)
    blk = pl.num_programs(3) - 1 - cb if rev else cb
    g0 = pl.multiple_of(blk * cl, cl)
    tri = tri_ref[int(rev)]
    negmask = negmask_ref[int(rev)]

    raw = dt_ref[0, 0, 0] + bias_ref[0, 0]
    dt = jnp.maximum(raw, 0.0) + jnp.log1p(jnp.exp(-jnp.abs(raw)))
    a = dt * -jnp.exp(alog_ref[0, 0])
    log_dt = jnp.log(dt)
    zero8 = jnp.zeros((hpg, CHUNK), F32)
    parts = []
    for k in range(SSD_CPS):
        parts += [p.astype(F32) for p in _split_bf16(a[:, k * CHUNK:(k + 1) * CHUNK], 3)] + [zero8]
    cs3 = jnp.dot(jnp.concatenate(parts, axis=0).astype(BF16), tri, preferred_element_type=F32)
    pad = jnp.zeros((CHUNK - TABLE_ROWS, CHUNK), F32)
    row_sub, tab_t = [], []
    for k in range(SSD_CPS):
        c3 = cs3[4 * hpg * k:4 * hpg * (k + 1)]
        cs = c3[0:hpg] + c3[hpg:2 * hpg] + c3[2 * hpg:3 * hpg]
        tot = cs[:, 0:1] if rev else cs[:, CHUNK - 1:CHUNK]
        w_end = jnp.exp(tot - cs) * dt[:, k * CHUNK:(k + 1) * CHUNK]
        w_in = jnp.exp(cs)
        tab_t.append(jnp.concatenate([cs, w_end, w_in, pad], axis=0).T)
        row_sub.append(cs - log_dt[:, k * CHUNK:(k + 1) * CHUNK])
    tab_t = jnp.concatenate(tab_t, axis=0)
    t_hi, t_lo = _split_bf16(tab_t, 2)
    fe = jnp.dot(jnp.concatenate([t_hi, t_lo], axis=1), ef_ref[...], preferred_element_type=F32)

    lane = lax.broadcasted_iota(jnp.int32, (CHUNK, GROUP_DIM), 1)
    even_head = (lane % (2 * HEAD_DIM)) < HEAD_DIM
    h = h_ref[...]
    for k in (reversed(range(SSD_CPS)) if rev else range(SSD_CPS)):
        rows = slice(k * CHUNK, (k + 1) * CHUNK)
        x = x_ref[rows, :]
        bmat = b_ref[rows, :]
        cmat = c_ref[rows, :]
        wend_e = fe[rows, 0:GROUP_DIM]
        win_e = fe[rows, GROUP_DIM:2 * GROUP_DIM]
        scores = lax.dot_general(cmat, bmat, (((1,), (1,)), ((), ())),
                                 preferred_element_type=F32)
        x_even = jnp.where(even_head, x, 0.0).astype(BF16)
        x_odd = jnp.where(even_head, 0.0, x).astype(BF16)
        tab_k = tab_t[rows]
        yd = []
        for j in range(hpg // 2):
            m_pair = []
            for hh in (2 * j, 2 * j + 1):
                seg = (jnp.broadcast_to(tab_k[:, ROW_CS + hh:ROW_CS + hh + 1], (CHUNK, CHUNK))
                       - row_sub[k][hh:hh + 1, :])
                m_pair.append((scores * jnp.exp(seg + negmask)).astype(BF16))
            cols = slice(2 * j * HEAD_DIM, 2 * (j + 1) * HEAD_DIM)
            rhs = jnp.concatenate([x_even[:, cols], x_odd[:, cols]], axis=0)
            yd.append(jnp.dot(jnp.concatenate(m_pair, axis=1), rhs, preferred_element_type=F32))
        y = (jnp.concatenate(yd, axis=1)
             + jnp.dot(cmat, h.astype(BF16), preferred_element_type=F32) * win_e)
        st = lax.dot_general(bmat, (x * wend_e).astype(BF16), (((0,), (0,)), ((), ())),
                             preferred_element_type=F32)
        chunk_decay = win_e[0:1, :] if rev else win_e[CHUNK - 1:CHUNK, :]
        h = h * chunk_decay + st

        grows = pl.ds(g0 + k * CHUNK, CHUNK)
        if not rev:
            yf_ref[grows, :] = y
        else:
            ytot = yf_ref[grows, :] + y + x * dskip_ref[...]
            gy = ytot * _silu(z_ref[rows, :])
            ms = jnp.mean(gy * gy, axis=-1, keepdims=True)
            o_ref[rows, :] = (gy * lax.rsqrt(ms + RMS_EPS) * nw_ref[...]).astype(o_ref.dtype)
    h_ref[...] = h


def _ssd_kernel(*refs):
    h_ref = refs[-2]

    @pl.when(pl.program_id(3) == 0)
    def _():
        h_ref[...] = jnp.zeros_like(h_ref)

    @pl.when(pl.program_id(2) == 0)
    def _():
        _ssd_block(False, *refs)

    @pl.when(pl.program_id(2) == 1)
    def _():
        _ssd_block(True, *refs)


def _ssd_constants():
    r = lax.broadcasted_iota(jnp.int32, (CHUNK, CHUNK), 0)
    c = lax.broadcasted_iota(jnp.int32, (CHUNK, CHUNK), 1)
    feeds = jnp.stack([c <= r, c >= r])
    negmask = jnp.where(feeds, 0.0, -jnp.inf).astype(F32)
    tri = jnp.swapaxes(feeds, 1, 2).astype(BF16)
    j = lax.broadcasted_iota(jnp.int32, (CHUNK, 2 * GROUP_DIM), 0)
    n = lax.broadcasted_iota(jnp.int32, (CHUNK, 2 * GROUP_DIM), 1)
    ef = (j == ROW_WEND + HEADS_PER_GROUP * (n // GROUP_DIM) + (n % GROUP_DIM) // HEAD_DIM).astype(BF16)
    return tri, negmask, jnp.concatenate([ef, ef], axis=0)


def _ssd_scan(xs, bc, zxbcdt, dt_t, dt_bias_all, a_log_all, d_skip_all, norm_w_all, layer, batch, seq):
    m = xs.shape[0]
    cl = SSD_CPS * CHUNK
    n_cb = seq // cl
    tri, negmask, ef = _ssd_constants()

    def in_blk(b, ph, c):
        return b * n_cb + c + ph * (n_cb - 1 - 2 * c)

    def out_blk(b, ph, c):
        return b * n_cb + n_cb - 1 - ph * c

    head_spec = pl.BlockSpec((None, 1, 1, HEADS_PER_GROUP, 1), lambda b, g, ph, c: (layer, ph, g, 0, 0))
    chan_spec = pl.BlockSpec((None, 1, GROUP_DIM), lambda b, g, ph, c: (layer, 0, g))
    return pl.pallas_call(
        _ssd_kernel,
        out_shape=jax.ShapeDtypeStruct((m, D_INNER), BF16),
        grid=(batch, N_GROUPS, 2, n_cb),
        in_specs=[
            pl.BlockSpec((cl, GROUP_DIM), lambda b, g, ph, c: (in_blk(b, ph, c), g)),
            pl.BlockSpec((cl, D_STATE), lambda b, g, ph, c: (in_blk(b, ph, c), g)),
            pl.BlockSpec((cl, D_STATE), lambda b, g, ph, c: (in_blk(b, ph, c), N_GROUPS + g)),
            pl.BlockSpec((cl, GROUP_DIM), lambda b, g, ph, c: (out_blk(b, ph, c), g)),
            pl.BlockSpec((1, 1, 1, HEADS_PER_GROUP, cl),
                         lambda b, g, ph, c: (b, ph, g, 0, c + ph * (n_cb - 1 - 2 * c))),
            head_spec, head_spec, chan_spec, chan_spec,
            pl.BlockSpec(tri.shape, lambda b, g, ph, c: (0, 0, 0)),
            pl.BlockSpec(negmask.shape, lambda b, g, ph, c: (0, 0, 0)),
            pl.BlockSpec(ef.shape, lambda b, g, ph, c: (0, 0)),
        ],
        out_specs=pl.BlockSpec((cl, GROUP_DIM), lambda b, g, ph, c: (out_blk(b, ph, c), g)),
        scratch_shapes=[pltpu.VMEM((D_STATE, GROUP_DIM), F32),
                        pltpu.VMEM((seq, GROUP_DIM), F32)],
        compiler_params=pltpu.CompilerParams(
            dimension_semantics=("arbitrary", "arbitrary", "arbitrary", "arbitrary"),
            vmem_limit_bytes=VMEM_LIMIT),
        name="ssd_scan",
    )(xs, bc, bc, zxbcdt, dt_t, dt_bias_all, a_log_all, d_skip_all, norm_w_all, tri, negmask, ef)


def _vec_spec(n, layer):
    return pl.BlockSpec((None, 1, n), lambda *_: (layer, 0, 0))


def _residual_norm_inplace(x_ref, g_ref, b_ref, o_ref):
    o_ref[...] = _layer_norm(DEEPNORM_ALPHA * x_ref[...] + o_ref[...], g_ref[...], b_ref[...])


def _outproj_kernel(a_ref, w_ref, x_ref, g_ref, b_ref, o_ref):
    j = pl.program_id(1)

    @pl.when(j == 0)
    def _():
        o_ref[...] = jnp.zeros_like(o_ref)

    o_ref[...] += jnp.dot(a_ref[...], w_ref[...].astype(BF16), preferred_element_type=F32)

    @pl.when(j == pl.num_programs(1) - 1)
    def _():
        _residual_norm_inplace(x_ref, g_ref, b_ref, o_ref)


def _out_proj_norm(gy, w_all, x, g_all, b_all, layer, norm_layer):
    m, k = gy.shape
    n = w_all.shape[2]
    row = lambda i, j: (i, 0)
    return pl.pallas_call(
        _outproj_kernel,
        out_shape=jax.ShapeDtypeStruct((m, n), F32),
        grid=(m // OUTPROJ_BM, k // OUTPROJ_BK),
        in_specs=[pl.BlockSpec((OUTPROJ_BM, OUTPROJ_BK), lambda i, j: (i, j)),
                  pl.BlockSpec((None, OUTPROJ_BK, n), lambda i, j: (layer, j, 0)),
                  pl.BlockSpec((OUTPROJ_BM, n), row, pipeline_mode=SINGLE_BUFFER),
                  _vec_spec(n, norm_layer), _vec_spec(n, norm_layer)],
        out_specs=pl.BlockSpec((OUTPROJ_BM, n), row),
        compiler_params=pltpu.CompilerParams(
            dimension_semantics=("parallel", "arbitrary"), vmem_limit_bytes=VMEM_LIMIT),
        name="out_proj_norm",
    )(gy, w_all, x, g_all, b_all)


def _pool_kernel(seq, x_ref, p_ref, n_ref, w_ref, bias_ref, scale_ref, g_ref, b_ref,
                 o_ref, ext_ref, v_ref):
    bt = x_ref.shape[0]
    seq_tiles = seq // bt
    pos = pl.program_id(0) % seq_tiles
    ext_ref[0:HALO, :] = jnp.where(pos == 0, 0.0, p_ref[...])
    ext_ref[HALO:HALO + bt, :] = x_ref[...]
    ext_ref[HALO + bt:2 * HALO + bt, :] = jnp.where(pos == seq_tiles - 1, 0.0, n_ref[...])
    t = pos * bt + lax.broadcasted_iota(jnp.int32, (bt, 1), 0)
    gd = POOL_GROUP_DIM
    for gi, win in enumerate(POOL_WINDOWS):
        cols = slice(gi * gd, (gi + 1) * gd)
        start = t - win // 2
        cnt = (jnp.clip(start + win, 0, seq) - jnp.clip(start, 0, seq)).astype(F32)
        wsum = ext_ref[pl.ds(HALO - win // 2, bt), cols]
        for k in range(1, win):
            wsum = wsum + ext_ref[pl.ds(HALO - win // 2 + k, bt), cols]
        xg = x_ref[:, cols]
        mg = (wsum / cnt - xg).astype(BF16)
        y = jnp.dot(mg, w_ref[gi].astype(BF16), preferred_element_type=F32) + bias_ref[:, cols]
        v_ref[:, cols] = DEEPNORM_ALPHA * xg + y * scale_ref[:, cols]
    o_ref[...] = _layer_norm(v_ref[...], g_ref[...], b_ref[...])


def _pool_mixer_norm(x, w_all, bias_all, scale_all, g_all, b_all, layer, norm_layer, seq):
    m, d = x.shape
    rb = POOL_BT // HALO
    n_halo_blocks = m // HALO
    row = lambda i: (i, 0)
    return pl.pallas_call(
        functools.partial(_pool_kernel, seq),
        out_shape=jax.ShapeDtypeStruct((m, d), F32),
        grid=(m // POOL_BT,),
        in_specs=[pl.BlockSpec((POOL_BT, d), row),
                  pl.BlockSpec((HALO, d), lambda i: (jnp.maximum(i * rb - 1, 0), 0)),
                  pl.BlockSpec((HALO, d), lambda i: (jnp.minimum((i + 1) * rb, n_halo_blocks - 1), 0)),
                  pl.BlockSpec((None,) + w_all.shape[1:], lambda i: (layer, 0, 0, 0)),
                  _vec_spec(d, layer), _vec_spec(d, layer),
                  _vec_spec(d, norm_layer), _vec_spec(d, norm_layer)],
        out_specs=pl.BlockSpec((POOL_BT, d), row),
        scratch_shapes=[pltpu.VMEM((POOL_BT + 2 * HALO, d), F32), pltpu.VMEM((POOL_BT, d), F32)],
        compiler_params=pltpu.CompilerParams(
            dimension_semantics=("parallel",), vmem_limit_bytes=VMEM_LIMIT),
        name="pool_mixer_norm",
    )(x, x, x, w_all, bias_all, scale_all, g_all, b_all)


def _mlp_kernel(x_ref, w1_ref, w2_ref, g_ref, b_ref, o_ref, xb_ref):
    j = pl.program_id(1)

    @pl.when(j == 0)
    def _():
        xb_ref[...] = x_ref[...].astype(BF16)
        o_ref[...] = jnp.zeros_like(o_ref)

    hid = jnp.dot(xb_ref[...], w1_ref[...].astype(BF16), preferred_element_type=F32)
    hid = jnp.square(jnp.maximum(hid, 0.0)).astype(BF16)
    o_ref[...] += jnp.dot(hid, w2_ref[...].astype(BF16), preferred_element_type=F32)

    @pl.when(j == pl.num_programs(1) - 1)
    def _():
        _residual_norm_inplace(x_ref, g_ref, b_ref, o_ref)


def _mlp_norm(x, w1_all, w2_all, g_all, b_all, layer):
    m, d = x.shape
    f = w1_all.shape[2]
    row = lambda i, j: (i, 0)
    return pl.pallas_call(
        _mlp_kernel,
        out_shape=jax.ShapeDtypeStruct((m, d), F32),
        grid=(m // MLP_BM, f // MLP_BF),
        in_specs=[pl.BlockSpec((MLP_BM, d), row, pipeline_mode=SINGLE_BUFFER),
                  pl.BlockSpec((None, d, MLP_BF), lambda i, j: (layer, 0, j)),
                  pl.BlockSpec((None, MLP_BF, d), lambda i, j: (layer, j, 0)),
                  _vec_spec(d, layer), _vec_spec(d, layer)],
        out_specs=pl.BlockSpec((MLP_BM, d), row),
        scratch_shapes=[pltpu.VMEM((MLP_BM, d), BF16)],
        compiler_params=pltpu.CompilerParams(
            dimension_semantics=("parallel", "arbitrary"), vmem_limit_bytes=VMEM_LIMIT),
        name="mlp_norm",
    )(x, w1_all, w2_all, g_all, b_all)


def kernel(x, ssd_in_proj, ssd_conv_w, ssd_conv_b, ssd_dt_bias, ssd_A_log, ssd_D, ssd_norm_w,
           ssd_out_proj, pool_w, pool_b, pool_scale, mlp_w1, mlp_w2, ln_mix_g, ln_mix_b,
           ln_ffn_g, ln_ffn_b):
    batch, seq, d = x.shape
    m = batch * seq
    assert d == D_MODEL and seq % (SSD_CPS * CHUNK) == 0 and seq % POOL_BT == 0 and seq % CONV_BT == 0
    n_ssd, n_pool = ssd_in_proj.shape[0], pool_w.shape[0]
    vecs = lambda v: v.reshape(v.shape[0], 1, -1)
    conv_w = ssd_conv_w.reshape(n_ssd, CONV_WIDTH, D_XBC)
    conv_b = vecs(ssd_conv_b)
    head_shape = (n_ssd, 2, N_GROUPS, HEADS_PER_GROUP, 1)
    dt_bias = ssd_dt_bias.reshape(head_shape)
    a_log = ssd_A_log.reshape(head_shape)
    d_skip = vecs(jnp.repeat(ssd_D, HEAD_DIM, axis=-1))
    norm_w = vecs(ssd_norm_w)
    pool_b, pool_scale = pool_b.reshape(n_pool, 1, d), vecs(pool_scale)
    ln_mix_g, ln_mix_b, ln_ffn_g, ln_ffn_b = (vecs(v) for v in (ln_mix_g, ln_mix_b, ln_ffn_g, ln_ffn_b))

    xf = x.reshape(m, d)
    for i in range(DEPTH):
        j = i // 2
        if i % 2 == 0:
            zxbcdt = _in_proj(xf.astype(BF16), ssd_in_proj, j)
            xs = _conv_silu(zxbcdt, conv_w, conv_b, j, seq, 0, D_INNER, F32)
            bc = _conv_silu(zxbcdt, conv_w, conv_b, j, seq, D_INNER, 2 * D_BC, BF16)
            dt_t = zxbcdt[:, D_INNER + D_XBC:].reshape(batch, seq, 2, N_GROUPS, HEADS_PER_GROUP)
            dt_t = jnp.transpose(dt_t, (0, 2, 3, 4, 1))
            gy = _ssd_scan(xs, bc, zxbcdt, dt_t, dt_bias, a_log, d_skip, norm_w, j, batch, seq)
            xf = _out_proj_norm(gy, ssd_out_proj, xf, ln_mix_g, ln_mix_b, j, i)
        else:
            xf = _pool_mixer_norm(xf, pool_w, pool_b, pool_scale, ln_mix_g, ln_mix_b, j, i, seq)
        xf = _mlp_norm(xf, mlp_w1, mlp_w2, ln_ffn_g, ln_ffn_b, i)
    return xf.reshape(batch, seq, d)
```

```python
import functools

import jax
import jax.numpy as jnp
from jax import lax
from jax.experimental import pallas as pl
from jax.experimental.pallas import tpu as pltpu

F32 = jnp.float32
BF16 = jnp.bfloat16

D_MODEL = 2048
DEPTH = 4
D_INNER = 2 * D_MODEL
HEAD_DIM = 64
N_HEADS = D_INNER // HEAD_DIM
N_GROUPS = 8
HEADS_PER_GROUP = N_HEADS // N_GROUPS
GROUP_DIM = D_INNER // N_GROUPS
D_STATE = 128
CONV_WIDTH = 5
CHUNK = 128
D_BC = N_GROUPS * D_STATE
D_XBC = D_INNER + 2 * D_BC
D_IN_PROJ = D_INNER + D_XBC + 2 * N_HEADS
POOL_WINDOWS = (2, 4, 8, 16)
POOL_GROUP_DIM = D_MODEL // len(POOL_WINDOWS)
D_FF = 4 * D_MODEL
DEEPNORM_ALPHA = (2.0 * DEPTH) ** 0.25
LN_EPS = 1e-5
RMS_EPS = 1e-5
LOG2_E = 1.4426950408889634

SUBLANES = 8
LANES = 128
VMEM_LIMIT = 56 * 1024 * 1024

MLP_BM = 1024
MLP_BF = 512
INPROJ_BM = 1024
INPROJ_BN = 1024
OUTPROJ_BM = 1024
OUTPROJ_BK = 1024
CONV_BC = 256
CONV_BR = 128
POOL_BT = 512
SSD_CPS = 8
HALO = SUBLANES
HALO_BF16 = 2 * SUBLANES
SINGLE_BUFFER = pl.Buffered(1)


def _layer_norm(v, g, b):
    mu = jnp.mean(v, axis=-1, keepdims=True)
    d = v - mu
    var = jnp.mean(d * d, axis=-1, keepdims=True)
    return d * lax.rsqrt(var + LN_EPS) * g + b


def _silu(v):
    return v * (1.0 / (1.0 + jnp.exp2(v * -LOG2_E)))


def _split_bf16(v, parts):
    out = []
    r = v
    for _ in range(parts):
        p = r.astype(BF16)
        out.append(p)
        r = r - p.astype(F32)
    return out


def _matmul_kernel(x_ref, w_ref, o_ref):
    o_ref[...] = jnp.dot(x_ref[...], w_ref[...].astype(BF16), preferred_element_type=F32)


def _in_proj(xb, w_all, layer, col_start, n_cols):
    m, k = xb.shape
    c0 = col_start // INPROJ_BN
    return pl.pallas_call(
        _matmul_kernel,
        out_shape=jax.ShapeDtypeStruct((m, n_cols), F32),
        grid=(m // INPROJ_BM, n_cols // INPROJ_BN),
        in_specs=[pl.BlockSpec((INPROJ_BM, k), lambda i, j: (i, 0)),
                  pl.BlockSpec((None, k, INPROJ_BN), lambda i, j: (layer, 0, j + c0))],
        out_specs=pl.BlockSpec((INPROJ_BM, INPROJ_BN), lambda i, j: (i, j)),
        compiler_params=pltpu.CompilerParams(
            dimension_semantics=("parallel", "parallel"), vmem_limit_bytes=VMEM_LIMIT),
        name="in_proj",
    )(xb, w_all)


def _dt_proj_kernel(w_ref, x_ref, o_ref):
    o_ref[0] = lax.dot_general(w_ref[...].astype(BF16), x_ref[...], (((1,), (1,)), ((), ())),
                               preferred_element_type=F32)


def _dt_proj(xb, w_dt_t_all, layer, batch, seq):
    m, k = xb.shape
    nh = w_dt_t_all.shape[1]
    seq_tiles = seq // INPROJ_BM
    return pl.pallas_call(
        _dt_proj_kernel,
        out_shape=jax.ShapeDtypeStruct((batch, nh, seq), F32),
        grid=(m // INPROJ_BM,),
        in_specs=[pl.BlockSpec((None, nh, k), lambda i: (layer, 0, 0)),
                  pl.BlockSpec((INPROJ_BM, k), lambda i: (i, 0))],
        out_specs=pl.BlockSpec((1, nh, INPROJ_BM), lambda i: (i // seq_tiles, 0, i % seq_tiles)),
        compiler_params=pltpu.CompilerParams(
            dimension_semantics=("parallel",), vmem_limit_bytes=VMEM_LIMIT),
        name="dt_proj",
    )(w_dt_t_all, xb)


def _in_proj_conv_kernel(seq_tiles, xm_ref, xp_ref, xn_ref, w_ref, cw_ref, cb_ref, o_ref, xext_ref):
    bt = xm_ref.shape[0]
    hb = HALO_BF16

    @pl.when(pl.program_id(1) == 0)
    def _():
        pos = pl.program_id(0) % seq_tiles
        xext_ref[0:hb, :] = jnp.where(pos == 0, jnp.zeros_like(xp_ref), xp_ref[...])
        xext_ref[hb:hb + bt, :] = xm_ref[...]
        xext_ref[hb + bt:2 * hb + bt, :] = jnp.where(pos == seq_tiles - 1, jnp.zeros_like(xn_ref),
                                                     xn_ref[...])

    pad = CONV_WIDTH // 2
    rb, halo = CONV_BR, HALO
    for c in range(o_ref.shape[1] // CONV_BC):
        cols = slice(c * CONV_BC, (c + 1) * CONV_BC)
        pre = jnp.dot(xext_ref[...], w_ref[:, cols].astype(BF16), preferred_element_type=F32)
        for r in range(0, bt, rb):
            blk = pre[hb + r - halo:hb + r + rb + halo]
            acc = cb_ref[:, cols] + cw_ref[pad:pad + 1, cols] * blk[halo:halo + rb]
            for k in range(CONV_WIDTH):
                if k != pad:
                    shifted = pltpu.roll(blk, (pad - k) % (rb + 2 * halo), axis=0)[halo:halo + rb]
                    acc = acc + cw_ref[k:k + 1, cols] * shifted
            o_ref[r:r + rb, cols] = _silu(acc).astype(o_ref.dtype)


def _in_proj_conv(xb, w_all, conv_w_all, conv_b_all, layer, seq, col_start, n_cols, out_dtype):
    m, k = xb.shape
    w0 = (D_INNER + col_start) // INPROJ_BN
    c0 = col_start // INPROJ_BN
    rb = INPROJ_BM // HALO_BF16
    n_halo_blocks = m // HALO_BF16
    return pl.pallas_call(
        functools.partial(_in_proj_conv_kernel, seq // INPROJ_BM),
        out_shape=jax.ShapeDtypeStruct((m, n_cols), out_dtype),
        grid=(m // INPROJ_BM, n_cols // INPROJ_BN),
        in_specs=[
            pl.BlockSpec((INPROJ_BM, k), lambda i, j: (i, 0)),
            pl.BlockSpec((HALO_BF16, k), lambda i, j: (jnp.maximum(i * rb - 1, 0), 0)),
            pl.BlockSpec((HALO_BF16, k), lambda i, j: (jnp.minimum((i + 1) * rb, n_halo_blocks - 1), 0)),
            pl.BlockSpec((None, k, INPROJ_BN), lambda i, j: (layer, 0, j + w0)),
            pl.BlockSpec((None, CONV_WIDTH, INPROJ_BN), lambda i, j: (layer, 0, j + c0)),
            pl.BlockSpec((None, 1, INPROJ_BN), lambda i, j: (layer, 0, j + c0)),
        ],
        out_specs=pl.BlockSpec((INPROJ_BM, INPROJ_BN), lambda i, j: (i, j)),
        scratch_shapes=[pltpu.VMEM((INPROJ_BM + 2 * HALO_BF16, k), BF16)],
        compiler_params=pltpu.CompilerParams(
            dimension_semantics=("parallel", "arbitrary"), vmem_limit_bytes=VMEM_LIMIT),
        name="in_proj_conv",
    )(xb, xb, xb, w_all, conv_w_all, conv_b_all)


ROW_CS, ROW_WEND, ROW_WIN = (HEADS_PER_GROUP * r for r in range(3))
TABLE_ROWS = 3 * HEADS_PER_GROUP


def _ssd_block(rev, x_ref, b_ref, c_ref, z_ref, dt_ref, bias_ref, alog_ref, dskip_ref, nw_ref,
               tri_ref, negmask_ref, ef_ref, o_ref, h_ref, yf_ref):
    hpg = HEADS_PER_GROUP
    cl = SSD_CPS * CHUNK
    cb = pl.program_id(3)
    blk = pl.num_programs(3) - 1 - cb if rev else cb
    g0 = pl.multiple_of(blk * cl, cl)
    tri = tri_ref[int(rev)]
    negmask = negmask_ref[int(rev)]

    raw = dt_ref[0, 0, 0] + bias_ref[0, 0]
    dt = jnp.maximum(raw, 0.0) + jnp.log1p(jnp.exp(-jnp.abs(raw)))
    a = dt * (-LOG2_E * jnp.exp(alog_ref[0, 0]))
    log_dt = jnp.log2(dt)
    zero8 = jnp.zeros((hpg, CHUNK), F32)
    parts = []
    for k in range(SSD_CPS):
        parts += [p.astype(F32) for p in _split_bf16(a[:, k * CHUNK:(k + 1) * CHUNK], 3)] + [zero8]
    cs3 = jnp.dot(jnp.concatenate(parts, axis=0).astype(BF16), tri, preferred_element_type=F32)
    pad = jnp.zeros((CHUNK - TABLE_ROWS, CHUNK), F32)
    row_sub, tab_t = [], []
    for k in range(SSD_CPS):
        c3 = cs3[4 * hpg * k:4 * hpg * (k + 1)]
        cs = c3[0:hpg] + c3[hpg:2 * hpg] + c3[2 * hpg:3 * hpg]
        tot = cs[:, 0:1] if rev else cs[:, CHUNK - 1:CHUNK]
        w_end = jnp.exp2(tot - cs) * dt[:, k * CHUNK:(k + 1) * CHUNK]
        w_in = jnp.exp2(cs)
        tab_t.append(jnp.concatenate([cs, w_end, w_in, pad], axis=0).T)
        row_sub.append(cs - log_dt[:, k * CHUNK:(k + 1) * CHUNK])
    tab_t = jnp.concatenate(tab_t, axis=0)
    t_hi, t_lo = _split_bf16(tab_t, 2)
    fe = jnp.dot(jnp.concatenate([t_hi, t_lo], axis=1), ef_ref[...], preferred_element_type=F32)

    lane = lax.broadcasted_iota(jnp.int32, (CHUNK, GROUP_DIM), 1)
    even_head = (lane % (2 * HEAD_DIM)) < HEAD_DIM
    h = h_ref[...]
    for k in (reversed(range(SSD_CPS)) if rev else range(SSD_CPS)):
        rows = slice(k * CHUNK, (k + 1) * CHUNK)
        x = x_ref[rows, :]
        bmat = b_ref[rows, :]
        cmat = c_ref[rows, :]
        wend_e = fe[rows, 0:GROUP_DIM]
        win_e = fe[rows, GROUP_DIM:2 * GROUP_DIM]
        scores = lax.dot_general(cmat, bmat, (((1,), (1,)), ((), ())),
                                 preferred_element_type=F32)
        x_even = jnp.where(even_head, x, 0.0).astype(BF16)
        x_odd = jnp.where(even_head, 0.0, x).astype(BF16)
        tab_k = tab_t[rows]
        yd = []
        for j in range(hpg // 2):
            m_pair = []
            for hh in (2 * j, 2 * j + 1):
                seg = (jnp.broadcast_to(tab_k[:, ROW_CS + hh:ROW_CS + hh + 1], (CHUNK, CHUNK))
                       - row_sub[k][hh:hh + 1, :])
                m_pair.append((scores * jnp.exp2(seg + negmask)).astype(BF16))
            cols = slice(2 * j * HEAD_DIM, 2 * (j + 1) * HEAD_DIM)
            rhs = jnp.concatenate([x_even[:, cols], x_odd[:, cols]], axis=0)
            yd.append(jnp.dot(jnp.concatenate(m_pair, axis=1), rhs, preferred_element_type=F32))
        y = (jnp.concatenate(yd, axis=1)
             + jnp.dot(cmat, h.astype(BF16), preferred_element_type=F32) * win_e)
        st = lax.dot_general(bmat, (x * wend_e).astype(BF16), (((0,), (0,)), ((), ())),
                             preferred_element_type=F32)
        chunk_decay = win_e[0:1, :] if rev else win_e[CHUNK - 1:CHUNK, :]
        h = h * chunk_decay + st

        grows = pl.ds(g0 + k * CHUNK, CHUNK)
        if not rev:
            yf_ref[grows, :] = y
        else:
            ytot = yf_ref[grows, :] + y + x * dskip_ref[...]
            gy = ytot * _silu(z_ref[rows, :])
            ms = jnp.mean(gy * gy, axis=-1, keepdims=True)
            o_ref[rows, :] = (gy * lax.rsqrt(ms + RMS_EPS) * nw_ref[...]).astype(o_ref.dtype)
    h_ref[...] = h


def _ssd_kernel(*refs):
    h_ref = refs[-2]

    @pl.when(pl.program_id(3) == 0)
    def _():
        h_ref[...] = jnp.zeros_like(h_ref)

    @pl.when(pl.program_id(2) == 0)
    def _():
        _ssd_block(False, *refs)

    @pl.when(pl.program_id(2) == 1)
    def _():
        _ssd_block(True, *refs)


def _ssd_constants():
    r = lax.broadcasted_iota(jnp.int32, (CHUNK, CHUNK), 0)
    c = lax.broadcasted_iota(jnp.int32, (CHUNK, CHUNK), 1)
    feeds = jnp.stack([c <= r, c >= r])
    negmask = jnp.where(feeds, 0.0, -jnp.inf).astype(F32)
    tri = jnp.swapaxes(feeds, 1, 2).astype(BF16)
    j = lax.broadcasted_iota(jnp.int32, (CHUNK, 2 * GROUP_DIM), 0)
    n = lax.broadcasted_iota(jnp.int32, (CHUNK, 2 * GROUP_DIM), 1)
    ef = (j == ROW_WEND + HEADS_PER_GROUP * (n // GROUP_DIM) + (n % GROUP_DIM) // HEAD_DIM).astype(BF16)
    return tri, negmask, jnp.concatenate([ef, ef], axis=0)


def _ssd_scan(xs, bc, z, dt_t, dt_bias_all, a_log_all, d_skip_all, norm_w_all, layer, batch, seq):
    m = xs.shape[0]
    cl = SSD_CPS * CHUNK
    n_cb = seq // cl
    tri, negmask, ef = _ssd_constants()

    def in_blk(b, ph, c):
        return b * n_cb + c + ph * (n_cb - 1 - 2 * c)

    def out_blk(b, ph, c):
        return b * n_cb + n_cb - 1 - ph * c

    head_spec = pl.BlockSpec((None, 1, 1, HEADS_PER_GROUP, 1), lambda b, g, ph, c: (layer, ph, g, 0, 0))
    chan_spec = pl.BlockSpec((None, 1, GROUP_DIM), lambda b, g, ph, c: (layer, 0, g))
    return pl.pallas_call(
        _ssd_kernel,
        out_shape=jax.ShapeDtypeStruct((m, D_INNER), BF16),
        grid=(batch, N_GROUPS, 2, n_cb),
        in_specs=[
            pl.BlockSpec((cl, GROUP_DIM), lambda b, g, ph, c: (in_blk(b, ph, c), g)),
            pl.BlockSpec((cl, D_STATE), lambda b, g, ph, c: (in_blk(b, ph, c), g)),
            pl.BlockSpec((cl, D_STATE), lambda b, g, ph, c: (in_blk(b, ph, c), N_GROUPS + g)),
            pl.BlockSpec((cl, GROUP_DIM), lambda b, g, ph, c: (out_blk(b, ph, c), g)),
            pl.BlockSpec((1, 1, 1, HEADS_PER_GROUP, cl),
                         lambda b, g, ph, c: (b, ph, g, 0, c + ph * (n_cb - 1 - 2 * c))),
            head_spec, head_spec, chan_spec, chan_spec,
            pl.BlockSpec(tri.shape, lambda b, g, ph, c: (0, 0, 0)),
            pl.BlockSpec(negmask.shape, lambda b, g, ph, c: (0, 0, 0)),
            pl.BlockSpec(ef.shape, lambda b, g, ph, c: (0, 0)),
        ],
        out_specs=pl.BlockSpec((cl, GROUP_DIM), lambda b, g, ph, c: (out_blk(b, ph, c), g)),
        scratch_shapes=[pltpu.VMEM((D_STATE, GROUP_DIM), F32),
                        pltpu.VMEM((seq, GROUP_DIM), F32)],
        compiler_params=pltpu.CompilerParams(
            dimension_semantics=("arbitrary", "arbitrary", "arbitrary", "arbitrary"),
            vmem_limit_bytes=VMEM_LIMIT),
        name="ssd_scan",
    )(xs, bc, bc, z, dt_t, dt_bias_all, a_log_all, d_skip_all, norm_w_all, tri, negmask, ef)


def _vec_spec(n, layer):
    return pl.BlockSpec((None, 1, n), lambda *_: (layer, 0, 0))


def _residual_norm_inplace(x_ref, g_ref, b_ref, o_ref):
    o_ref[...] = _layer_norm(DEEPNORM_ALPHA * x_ref[...] + o_ref[...], g_ref[...], b_ref[...])


def _outproj_kernel(a_ref, w_ref, x_ref, g_ref, b_ref, o_ref):
    j = pl.program_id(1)

    @pl.when(j == 0)
    def _():
        o_ref[...] = jnp.zeros_like(o_ref)

    o_ref[...] += jnp.dot(a_ref[...], w_ref[...], preferred_element_type=F32)

    @pl.when(j == pl.num_programs(1) - 1)
    def _():
        _residual_norm_inplace(x_ref, g_ref, b_ref, o_ref)


def _out_proj_norm(gy, w_all, x, g_all, b_all, layer, norm_layer):
    m, k = gy.shape
    n = w_all.shape[2]
    row = lambda i, j: (i, 0)
    return pl.pallas_call(
        _outproj_kernel,
        out_shape=jax.ShapeDtypeStruct((m, n), F32),
        grid=(m // OUTPROJ_BM, k // OUTPROJ_BK),
        in_specs=[pl.BlockSpec((OUTPROJ_BM, OUTPROJ_BK), lambda i, j: (i, j)),
                  pl.BlockSpec((None, OUTPROJ_BK, n), lambda i, j: (layer, j, 0)),
                  pl.BlockSpec((OUTPROJ_BM, n), row),
                  _vec_spec(n, norm_layer), _vec_spec(n, norm_layer)],
        out_specs=pl.BlockSpec((OUTPROJ_BM, n), row),
        compiler_params=pltpu.CompilerParams(
            dimension_semantics=("parallel", "arbitrary"), vmem_limit_bytes=VMEM_LIMIT),
        name="out_proj_norm",
    )(gy, w_all, x, g_all, b_all)


def _pool_kernel(seq, x_ref, p_ref, n_ref, w_ref, bias_ref, scale_ref, g_ref, b_ref,
                 o_ref, v_ref):
    bt = x_ref.shape[0]
    n_ext = bt + 2 * HALO
    seq_tiles = seq // bt
    pos = pl.program_id(0) % seq_tiles
    t = pos * bt + lax.broadcasted_iota(jnp.int32, (bt, 1), 0)
    gd = POOL_GROUP_DIM
    shift_down = lambda v, s: pltpu.roll(v, s % n_ext, axis=0)
    for gi, win in enumerate(POOL_WINDOWS):
        cols = slice(gi * gd, (gi + 1) * gd)
        start = t - win // 2
        cnt = (jnp.clip(start + win, 0, seq) - jnp.clip(start, 0, seq)).astype(F32)
        ext = jnp.concatenate([jnp.where(pos == 0, 0.0, p_ref[:, cols]), x_ref[:, cols],
                               jnp.where(pos == seq_tiles - 1, 0.0, n_ref[:, cols])], axis=0)
        psum, span = ext, 1
        while span < win:
            psum = psum + shift_down(psum, span)
            span *= 2
        ahead = win - win // 2 - 1
        wsum = (shift_down(psum, -ahead) if ahead else psum)[HALO:HALO + bt]
        xg = x_ref[:, cols]
        mg = (wsum / cnt - xg).astype(BF16)
        y = jnp.dot(mg, w_ref[gi].astype(BF16), preferred_element_type=F32) + bias_ref[:, cols]
        v_ref[:, cols] = DEEPNORM_ALPHA * xg + y * scale_ref[:, cols]
    o_ref[...] = _layer_norm(v_ref[...], g_ref[...], b_ref[...])


def _pool_mixer_norm(x, w_all, bias_all, scale_all, g_all, b_all, layer, norm_layer, seq):
    m, d = x.shape
    rb = POOL_BT // HALO
    n_halo_blocks = m // HALO
    row = lambda i: (i, 0)
    return pl.pallas_call(
        functools.partial(_pool_kernel, seq),
        out_shape=jax.ShapeDtypeStruct((m, d), F32),
        grid=(m // POOL_BT,),
        in_specs=[pl.BlockSpec((POOL_BT, d), row),
                  pl.BlockSpec((HALO, d), lambda i: (jnp.maximum(i * rb - 1, 0), 0)),
                  pl.BlockSpec((HALO, d), lambda i: (jnp.minimum((i + 1) * rb, n_halo_blocks - 1), 0)),
                  pl.BlockSpec((None,) + w_all.shape[1:], lambda i: (layer, 0, 0, 0)),
                  _vec_spec(d, layer), _vec_spec(d, layer),
                  _vec_spec(d, norm_layer), _vec_spec(d, norm_layer)],
        out_specs=pl.BlockSpec((POOL_BT, d), row),
        scratch_shapes=[pltpu.VMEM((POOL_BT, d), F32)],
        compiler_params=pltpu.CompilerParams(
            dimension_semantics=("parallel",), vmem_limit_bytes=VMEM_LIMIT),
        name="pool_mixer_norm",
    )(x, x, x, w_all, bias_all, scale_all, g_all, b_all)


def _mlp_kernel(x_ref, w1_ref, w2_ref, g_ref, b_ref, o_ref, xb_ref):
    j = pl.program_id(1)

    @pl.when(j == 0)
    def _():
        xb_ref[...] = x_ref[...].astype(BF16)
        o_ref[...] = jnp.zeros_like(o_ref)

    hid = jnp.dot(xb_ref[...], w1_ref[...].astype(BF16), preferred_element_type=F32)
    hid = jnp.square(jnp.maximum(hid, 0.0)).astype(BF16)
    o_ref[...] += jnp.dot(hid, w2_ref[...].astype(BF16), preferred_element_type=F32)

    @pl.when(j == pl.num_programs(1) - 1)
    def _():
        _residual_norm_inplace(x_ref, g_ref, b_ref, o_ref)


def _mlp_norm(x, w1_all, w2_all, g_all, b_all, layer):
    m, d = x.shape
    f = w1_all.shape[2]
    row = lambda i, j: (i, 0)
    return pl.pallas_call(
        _mlp_kernel,
        out_shape=jax.ShapeDtypeStruct((m, d), F32),
        grid=(m // MLP_BM, f // MLP_BF),
        in_specs=[pl.BlockSpec((MLP_BM, d), row, pipeline_mode=SINGLE_BUFFER),
                  pl.BlockSpec((None, d, MLP_BF), lambda i, j: (layer, 0, j)),
                  pl.BlockSpec((None, MLP_BF, d), lambda i, j: (layer, j, 0)),
                  _vec_spec(d, layer), _vec_spec(d, layer)],
        out_specs=pl.BlockSpec((MLP_BM, d), row),
        scratch_shapes=[pltpu.VMEM((MLP_BM, d), BF16)],
        compiler_params=pltpu.CompilerParams(
            dimension_semantics=("parallel", "arbitrary"), vmem_limit_bytes=VMEM_LIMIT),
        name="mlp_norm",
    )(x, w1_all, w2_all, g_all, b_all)


def kernel(x, ssd_in_proj, ssd_conv_w, ssd_conv_b, ssd_dt_bias, ssd_A_log, ssd_D, ssd_norm_w,
           ssd_out_proj, pool_w, pool_b, pool_scale, mlp_w1, mlp_w2, ln_mix_g, ln_mix_b,
           ln_ffn_g, ln_ffn_b):
    batch, seq, d = x.shape
    m = batch * seq
    assert d == D_MODEL and seq % (SSD_CPS * CHUNK) == 0 and seq % POOL_BT == 0 and seq % INPROJ_BM == 0
    n_ssd, n_pool = ssd_in_proj.shape[0], pool_w.shape[0]
    vecs = lambda v: v.reshape(v.shape[0], 1, -1)
    w_dt_t = jnp.swapaxes(ssd_in_proj[:, :, D_INNER + D_XBC:], 1, 2)
    w_out = ssd_out_proj.astype(BF16)
    conv_w = ssd_conv_w.reshape(n_ssd, CONV_WIDTH, D_XBC)
    conv_b = vecs(ssd_conv_b)
    head_shape = (n_ssd, 2, N_GROUPS, HEADS_PER_GROUP, 1)
    dt_bias = ssd_dt_bias.reshape(head_shape)
    a_log = ssd_A_log.reshape(head_shape)
    d_skip = vecs(jnp.repeat(ssd_D, HEAD_DIM, axis=-1))
    norm_w = vecs(ssd_norm_w)
    pool_b, pool_scale = pool_b.reshape(n_pool, 1, d), vecs(pool_scale)
    ln_mix_g, ln_mix_b, ln_ffn_g, ln_ffn_b = (vecs(v) for v in (ln_mix_g, ln_mix_b, ln_ffn_g, ln_ffn_b))

    xf = x.reshape(m, d)
    for i in range(DEPTH):
        j = i // 2
        if i % 2 == 0:
            xb = xf.astype(BF16)
            z = _in_proj(xb, ssd_in_proj, j, 0, D_INNER)
            xs = _in_proj_conv(xb, ssd_in_proj, conv_w, conv_b, j, seq, 0, D_INNER, F32)
            bc = _in_proj_conv(xb, ssd_in_proj, conv_w, conv_b, j, seq, D_INNER, 2 * D_BC, BF16)
            dt_t = _dt_proj(xb, w_dt_t, j, batch, seq).reshape(batch, 2, N_GROUPS, HEADS_PER_GROUP, seq)
            gy = _ssd_scan(xs, bc, z, dt_t, dt_bias, a_log, d_skip, norm_w, j, batch, seq)
            xf = _out_proj_norm(gy, w_out, xf, ln_mix_g, ln_mix_b, j, i)
        else:
            xf = _pool_mixer_norm(xf, pool_w, pool_b, pool_scale, ln_mix_g, ln_mix_b, j, i, seq)
        xf = _mlp_norm(xf, mlp_w1, mlp_w2, ln_ffn_g, ln_ffn_b, i)
    return xf.reshape(batch, seq, d)
```

```python
import functools

import jax
import jax.numpy as jnp
from jax import lax
from jax.experimental import pallas as pl
from jax.experimental.pallas import tpu as pltpu

F32 = jnp.float32
BF16 = jnp.bfloat16

D_MODEL = 2048
DEPTH = 4
D_INNER = 2 * D_MODEL
HEAD_DIM = 64
N_HEADS = D_INNER // HEAD_DIM
N_GROUPS = 8
HEADS_PER_GROUP = N_HEADS // N_GROUPS
GROUP_DIM = D_INNER // N_GROUPS
D_STATE = 128
CONV_WIDTH = 5
CHUNK = 128
D_BC = N_GROUPS * D_STATE
D_XBC = D_INNER + 2 * D_BC
D_IN_PROJ = D_INNER + D_XBC + 2 * N_HEADS
POOL_WINDOWS = (2, 4, 8, 16)
POOL_GROUP_DIM = D_MODEL // len(POOL_WINDOWS)
D_FF = 4 * D_MODEL
DEEPNORM_ALPHA = (2.0 * DEPTH) ** 0.25
LN_EPS = 1e-5
RMS_EPS = 1e-5
LOG2_E = 1.4426950408889634

SUBLANES = 8
LANES = 128
VMEM_LIMIT = 60 * 1024 * 1024

MLP_BM = 1024
MLP_BF = 512
INPROJ_BM = 1024
INPROJ_BN = 1024
OUTPROJ_BM = 1024
OUTPROJ_BK = 1024
CONV_BC = 256
CONV_BR = 128
POOL_BT = 512
SSD_CPS = 8
HALO = SUBLANES
HALO_BF16 = 2 * SUBLANES
SINGLE_BUFFER = pl.Buffered(1)


def _layer_norm(v, g, b):
    mu = jnp.mean(v, axis=-1, keepdims=True)
    d = v - mu
    var = jnp.mean(d * d, axis=-1, keepdims=True)
    return d * lax.rsqrt(var + LN_EPS) * g + b


def _silu(v):
    return v * (1.0 / (1.0 + jnp.exp2(v * -LOG2_E)))


def _split_bf16(v, parts):
    out = []
    r = v
    for _ in range(parts):
        p = r.astype(BF16)
        out.append(p)
        r = r - p.astype(F32)
    return out


def _matmul_kernel(x_ref, w_ref, o_ref):
    o_ref[...] = jnp.dot(x_ref[...], w_ref[...].astype(BF16), preferred_element_type=F32)


def _in_proj(xb, w_all, layer, col_start, n_cols):
    m, k = xb.shape
    c0 = col_start // INPROJ_BN
    return pl.pallas_call(
        _matmul_kernel,
        out_shape=jax.ShapeDtypeStruct((m, n_cols), F32),
        grid=(n_cols // INPROJ_BN, m // INPROJ_BM),
        in_specs=[pl.BlockSpec((INPROJ_BM, k), lambda j, i: (i, 0)),
                  pl.BlockSpec((None, k, INPROJ_BN), lambda j, i: (layer, 0, j + c0))],
        out_specs=pl.BlockSpec((INPROJ_BM, INPROJ_BN), lambda j, i: (i, j)),
        compiler_params=pltpu.CompilerParams(
            dimension_semantics=("parallel", "parallel"), vmem_limit_bytes=VMEM_LIMIT),
        name="in_proj",
    )(xb, w_all)


def _dt_proj_kernel(w_ref, x_ref, o_ref):
    o_ref[0] = lax.dot_general(w_ref[...].astype(BF16), x_ref[...], (((1,), (1,)), ((), ())),
                               preferred_element_type=F32)


def _dt_proj(xb, w_dt_t_all, layer, batch, seq):
    m, k = xb.shape
    nh = w_dt_t_all.shape[1]
    seq_tiles = seq // INPROJ_BM
    return pl.pallas_call(
        _dt_proj_kernel,
        out_shape=jax.ShapeDtypeStruct((batch, nh, seq), F32),
        grid=(m // INPROJ_BM,),
        in_specs=[pl.BlockSpec((None, nh, k), lambda i: (layer, 0, 0)),
                  pl.BlockSpec((INPROJ_BM, k), lambda i: (i, 0))],
        out_specs=pl.BlockSpec((1, nh, INPROJ_BM), lambda i: (i // seq_tiles, 0, i % seq_tiles)),
        compiler_params=pltpu.CompilerParams(
            dimension_semantics=("parallel",), vmem_limit_bytes=VMEM_LIMIT),
        name="dt_proj",
    )(w_dt_t_all, xb)


def _in_proj_conv_kernel(seq_tiles, xm_ref, xp_ref, xn_ref, w_ref, cw_ref, cb_ref, o_ref, xext_ref):
    bt = xm_ref.shape[0]
    hb = HALO_BF16

    @pl.when(pl.program_id(1) == 0)
    def _():
        pos = pl.program_id(0) % seq_tiles
        xext_ref[0:hb, :] = jnp.where(pos == 0, jnp.zeros_like(xp_ref), xp_ref[...])
        xext_ref[hb:hb + bt, :] = xm_ref[...]
        xext_ref[hb + bt:2 * hb + bt, :] = jnp.where(pos == seq_tiles - 1, jnp.zeros_like(xn_ref),
                                                     xn_ref[...])

    pad = CONV_WIDTH // 2
    rb, halo = CONV_BR, HALO
    for c in range(o_ref.shape[1] // CONV_BC):
        cols = slice(c * CONV_BC, (c + 1) * CONV_BC)
        pre = jnp.dot(xext_ref[...], w_ref[:, cols].astype(BF16), preferred_element_type=F32)
        for r in range(0, bt, rb):
            blk = pre[hb + r - halo:hb + r + rb + halo]
            acc = cb_ref[:, cols] + cw_ref[pad:pad + 1, cols] * blk[halo:halo + rb]
            for k in range(CONV_WIDTH):
                if k != pad:
                    shifted = pltpu.roll(blk, (pad - k) % (rb + 2 * halo), axis=0)[halo:halo + rb]
                    acc = acc + cw_ref[k:k + 1, cols] * shifted
            o_ref[r:r + rb, cols] = _silu(acc).astype(o_ref.dtype)


def _in_proj_conv(xb, w_all, conv_w_all, conv_b_all, layer, seq, col_start, n_cols, out_dtype):
    m, k = xb.shape
    w0 = (D_INNER + col_start) // INPROJ_BN
    c0 = col_start // INPROJ_BN
    rb = INPROJ_BM // HALO_BF16
    n_halo_blocks = m // HALO_BF16
    return pl.pallas_call(
        functools.partial(_in_proj_conv_kernel, seq // INPROJ_BM),
        out_shape=jax.ShapeDtypeStruct((m, n_cols), out_dtype),
        grid=(m // INPROJ_BM, n_cols // INPROJ_BN),
        in_specs=[
            pl.BlockSpec((INPROJ_BM, k), lambda i, j: (i, 0)),
            pl.BlockSpec((HALO_BF16, k), lambda i, j: (jnp.maximum(i * rb - 1, 0), 0)),
            pl.BlockSpec((HALO_BF16, k), lambda i, j: (jnp.minimum((i + 1) * rb, n_halo_blocks - 1), 0)),
            pl.BlockSpec((None, k, INPROJ_BN), lambda i, j: (layer, 0, j + w0)),
            pl.BlockSpec((None, CONV_WIDTH, INPROJ_BN), lambda i, j: (layer, 0, j + c0)),
            pl.BlockSpec((None, 1, INPROJ_BN), lambda i, j: (layer, 0, j + c0)),
        ],
        out_specs=pl.BlockSpec((INPROJ_BM, INPROJ_BN), lambda i, j: (i, j)),
        scratch_shapes=[pltpu.VMEM((INPROJ_BM + 2 * HALO_BF16, k), BF16)],
        compiler_params=pltpu.CompilerParams(
            dimension_semantics=("parallel", "arbitrary"), vmem_limit_bytes=VMEM_LIMIT),
        name="in_proj_conv",
    )(xb, xb, xb, w_all, conv_w_all, conv_b_all)


ROW_CS, ROW_WEND, ROW_WIN = (HEADS_PER_GROUP * r for r in range(3))
TABLE_ROWS = 3 * HEADS_PER_GROUP


def _ssd_block(rev, x_ref, b_ref, c_ref, z_ref, dt_ref, bias_ref, alog_ref, dskip_ref, nw_ref,
               tri_ref, negmask_ref, ef_ref, o_ref, h_ref, yf_ref):
    hpg = HEADS_PER_GROUP
    cl = SSD_CPS * CHUNK
    cb = pl.program_id(3)
    blk = pl.num_programs(3) - 1 - cb if rev else cb
    g0 = pl.multiple_of(blk * cl, cl)
    tri = tri_ref[int(rev)]
    negmask = negmask_ref[int(rev)]

    raw = dt_ref[0, 0, 0] + bias_ref[0, 0]
    dt = jnp.maximum(raw, 0.0) + jnp.log1p(jnp.exp(-jnp.abs(raw)))
    a = dt * (-LOG2_E * jnp.exp(alog_ref[0, 0]))
    log_dt = jnp.log2(dt)
    zero8 = jnp.zeros((hpg, CHUNK), F32)
    parts = []
    for k in range(SSD_CPS):
        parts += [p.astype(F32) for p in _split_bf16(a[:, k * CHUNK:(k + 1) * CHUNK], 3)] + [zero8]
    cs3 = jnp.dot(jnp.concatenate(parts, axis=0).astype(BF16), tri, preferred_element_type=F32)
    pad = jnp.zeros((CHUNK - TABLE_ROWS, CHUNK), F32)
    row_sub, tab_t = [], []
    for k in range(SSD_CPS):
        c3 = cs3[4 * hpg * k:4 * hpg * (k + 1)]
        cs = c3[0:hpg] + c3[hpg:2 * hpg] + c3[2 * hpg:3 * hpg]
        tot = cs[:, 0:1] if rev else cs[:, CHUNK - 1:CHUNK]
        w_end = jnp.exp2(tot - cs) * dt[:, k * CHUNK:(k + 1) * CHUNK]
        w_in = jnp.exp2(cs)
        tab_t.append(jnp.concatenate([cs, w_end, w_in, pad], axis=0).T)
        row_sub.append(cs - log_dt[:, k * CHUNK:(k + 1) * CHUNK])
    tab_t = jnp.concatenate(tab_t, axis=0)
    t_hi, t_lo = _split_bf16(tab_t, 2)
    fe = jnp.dot(jnp.concatenate([t_hi, t_lo], axis=1), ef_ref[...], preferred_element_type=F32)

    lane = lax.broadcasted_iota(jnp.int32, (CHUNK, GROUP_DIM), 1)
    even_head = (lane % (2 * HEAD_DIM)) < HEAD_DIM
    h = h_ref[...]
    for k in (reversed(range(SSD_CPS)) if rev else range(SSD_CPS)):
        rows = slice(k * CHUNK, (k + 1) * CHUNK)
        x = x_ref[rows, :]
        bmat = b_ref[rows, :]
        cmat = c_ref[rows, :]
        wend_e = fe[rows, 0:GROUP_DIM]
        win_e = fe[rows, GROUP_DIM:2 * GROUP_DIM]
        scores = lax.dot_general(cmat, bmat, (((1,), (1,)), ((), ())),
                                 preferred_element_type=F32)
        x_even = jnp.where(even_head, x, 0.0).astype(BF16)
        x_odd = jnp.where(even_head, 0.0, x).astype(BF16)
        tab_k = tab_t[rows]
        yd = []
        for j in range(hpg // 2):
            m_pair = []
            for hh in (2 * j, 2 * j + 1):
                seg = (jnp.broadcast_to(tab_k[:, ROW_CS + hh:ROW_CS + hh + 1], (CHUNK, CHUNK))
                       - row_sub[k][hh:hh + 1, :])
                m_pair.append((scores * jnp.exp2(seg + negmask)).astype(BF16))
            cols = slice(2 * j * HEAD_DIM, 2 * (j + 1) * HEAD_DIM)
            rhs = jnp.concatenate([x_even[:, cols], x_odd[:, cols]], axis=0)
            yd.append(jnp.dot(jnp.concatenate(m_pair, axis=1), rhs, preferred_element_type=F32))
        y = (jnp.concatenate(yd, axis=1)
             + jnp.dot(cmat, h.astype(BF16), preferred_element_type=F32) * win_e)
        st = lax.dot_general(bmat, (x * wend_e).astype(BF16), (((0,), (0,)), ((), ())),
                             preferred_element_type=F32)
        chunk_decay = win_e[0:1, :] if rev else win_e[CHUNK - 1:CHUNK, :]
        h = h * chunk_decay + st

        grows = pl.ds(g0 + k * CHUNK, CHUNK)
        if not rev:
            yf_ref[grows, :] = y
        else:
            ytot = yf_ref[grows, :] + y + x * dskip_ref[...]
            gy = ytot * _silu(z_ref[rows, :])
            ms = jnp.mean(gy * gy, axis=-1, keepdims=True)
            o_ref[rows, :] = (gy * lax.rsqrt(ms + RMS_EPS) * nw_ref[...]).astype(o_ref.dtype)
    h_ref[...] = h


def _ssd_kernel(*refs):
    h_ref = refs[-2]

    @pl.when(pl.program_id(3) == 0)
    def _():
        h_ref[...] = jnp.zeros_like(h_ref)

    @pl.when(pl.program_id(2) == 0)
    def _():
        _ssd_block(False, *refs)

    @pl.when(pl.program_id(2) == 1)
    def _():
        _ssd_block(True, *refs)


def _ssd_constants():
    r = lax.broadcasted_iota(jnp.int32, (CHUNK, CHUNK), 0)
    c = lax.broadcasted_iota(jnp.int32, (CHUNK, CHUNK), 1)
    feeds = jnp.stack([c <= r, c >= r])
    negmask = jnp.where(feeds, 0.0, -jnp.inf).astype(F32)
    tri = jnp.swapaxes(feeds, 1, 2).astype(BF16)
    j = lax.broadcasted_iota(jnp.int32, (CHUNK, 2 * GROUP_DIM), 0)
    n = lax.broadcasted_iota(jnp.int32, (CHUNK, 2 * GROUP_DIM), 1)
    ef = (j == ROW_WEND + HEADS_PER_GROUP * (n // GROUP_DIM) + (n % GROUP_DIM) // HEAD_DIM).astype(BF16)
    return tri, negmask, jnp.concatenate([ef, ef], axis=0)


def _ssd_scan(xs, bc, z, dt_t, dt_bias_all, a_log_all, d_skip_all, norm_w_all, layer, batch, seq):
    m = xs.shape[0]
    cl = SSD_CPS * CHUNK
    n_cb = seq // cl
    tri, negmask, ef = _ssd_constants()

    def in_blk(b, ph, c):
        return b * n_cb + c + ph * (n_cb - 1 - 2 * c)

    def out_blk(b, ph, c):
        return b * n_cb + n_cb - 1 - ph * c

    head_spec = pl.BlockSpec((None, 1, 1, HEADS_PER_GROUP, 1), lambda b, g, ph, c: (layer, ph, g, 0, 0))
    chan_spec = pl.BlockSpec((None, 1, GROUP_DIM), lambda b, g, ph, c: (layer, 0, g))
    return pl.pallas_call(
        _ssd_kernel,
        out_shape=jax.ShapeDtypeStruct((m, D_INNER), BF16),
        grid=(batch, N_GROUPS, 2, n_cb),
        in_specs=[
            pl.BlockSpec((cl, GROUP_DIM), lambda b, g, ph, c: (in_blk(b, ph, c), g)),
            pl.BlockSpec((cl, D_STATE), lambda b, g, ph, c: (in_blk(b, ph, c), g)),
            pl.BlockSpec((cl, D_STATE), lambda b, g, ph, c: (in_blk(b, ph, c), N_GROUPS + g)),
            pl.BlockSpec((cl, GROUP_DIM), lambda b, g, ph, c: (out_blk(b, ph, c), g)),
            pl.BlockSpec((1, 1, 1, HEADS_PER_GROUP, cl),
                         lambda b, g, ph, c: (b, ph, g, 0, c + ph * (n_cb - 1 - 2 * c))),
            head_spec, head_spec, chan_spec, chan_spec,
            pl.BlockSpec(tri.shape, lambda b, g, ph, c: (0, 0, 0)),
            pl.BlockSpec(negmask.shape, lambda b, g, ph, c: (0, 0, 0)),
            pl.BlockSpec(ef.shape, lambda b, g, ph, c: (0, 0)),
        ],
        out_specs=pl.BlockSpec((cl, GROUP_DIM), lambda b, g, ph, c: (out_blk(b, ph, c), g)),
        scratch_shapes=[pltpu.VMEM((D_STATE, GROUP_DIM), F32),
                        pltpu.VMEM((seq, GROUP_DIM), F32)],
        compiler_params=pltpu.CompilerParams(
            dimension_semantics=("arbitrary", "arbitrary", "arbitrary", "arbitrary"),
            vmem_limit_bytes=VMEM_LIMIT),
        name="ssd_scan",
    )(xs, bc, bc, z, dt_t, dt_bias_all, a_log_all, d_skip_all, norm_w_all, tri, negmask, ef)


def _vec_spec(n, layer):
    return pl.BlockSpec((None, 1, n), lambda *_: (layer, 0, 0))


def _residual_norm_inplace(x_ref, g_ref, b_ref, o_ref):
    o_ref[...] = _layer_norm(DEEPNORM_ALPHA * x_ref[...] + o_ref[...], g_ref[...], b_ref[...])


def _outproj_kernel(a_ref, w_ref, x_ref, g_ref, b_ref, o_ref):
    j = pl.program_id(1)

    @pl.when(j == 0)
    def _():
        o_ref[...] = jnp.zeros_like(o_ref)

    o_ref[...] += jnp.dot(a_ref[...], w_ref[...], preferred_element_type=F32)

    @pl.when(j == pl.num_programs(1) - 1)
    def _():
        _residual_norm_inplace(x_ref, g_ref, b_ref, o_ref)


def _out_proj_norm(gy, w_all, x, g_all, b_all, layer, norm_layer):
    m, k = gy.shape
    n = w_all.shape[2]
    row = lambda i, j: (i, 0)
    return pl.pallas_call(
        _outproj_kernel,
        out_shape=jax.ShapeDtypeStruct((m, n), F32),
        grid=(m // OUTPROJ_BM, k // OUTPROJ_BK),
        in_specs=[pl.BlockSpec((OUTPROJ_BM, OUTPROJ_BK), lambda i, j: (i, j)),
                  pl.BlockSpec((None, OUTPROJ_BK, n), lambda i, j: (layer, j, 0)),
                  pl.BlockSpec((OUTPROJ_BM, n), row),
                  _vec_spec(n, norm_layer), _vec_spec(n, norm_layer)],
        out_specs=pl.BlockSpec((OUTPROJ_BM, n), row),
        compiler_params=pltpu.CompilerParams(
            dimension_semantics=("parallel", "arbitrary"), vmem_limit_bytes=VMEM_LIMIT),
        name="out_proj_norm",
    )(gy, w_all, x, g_all, b_all)


def _pool_kernel(seq, x_ref, p_ref, n_ref, w_ref, bias_ref, scale_ref, g_ref, b_ref,
                 o_ref, v_ref):
    bt = x_ref.shape[0]
    n_ext = bt + 2 * HALO
    seq_tiles = seq // bt
    pos = pl.program_id(0) % seq_tiles
    t = pos * bt + lax.broadcasted_iota(jnp.int32, (bt, 1), 0)
    gd = POOL_GROUP_DIM
    shift_down = lambda v, s: pltpu.roll(v, s % n_ext, axis=0)
    for gi, win in enumerate(POOL_WINDOWS):
        cols = slice(gi * gd, (gi + 1) * gd)
        start = t - win // 2
        cnt = (jnp.clip(start + win, 0, seq) - jnp.clip(start, 0, seq)).astype(F32)
        ext = jnp.concatenate([jnp.where(pos == 0, 0.0, p_ref[:, cols]), x_ref[:, cols],
                               jnp.where(pos == seq_tiles - 1, 0.0, n_ref[:, cols])], axis=0)
        psum, span = ext, 1
        while span < win:
            psum = psum + shift_down(psum, span)
            span *= 2
        ahead = win - win // 2 - 1
        wsum = (shift_down(psum, -ahead) if ahead else psum)[HALO:HALO + bt]
        xg = x_ref[:, cols]
        mg = (wsum / cnt - xg).astype(BF16)
        y = jnp.dot(mg, w_ref[gi].astype(BF16), preferred_element_type=F32) + bias_ref[:, cols]
        v_ref[:, cols] = DEEPNORM_ALPHA * xg + y * scale_ref[:, cols]
    o_ref[...] = _layer_norm(v_ref[...], g_ref[...], b_ref[...])


def _pool_mixer_norm(x, w_all, bias_all, scale_all, g_all, b_all, layer, norm_layer, seq):
    m, d = x.shape
    rb = POOL_BT // HALO
    n_halo_blocks = m // HALO
    row = lambda i: (i, 0)
    return pl.pallas_call(
        functools.partial(_pool_kernel, seq),
        out_shape=jax.ShapeDtypeStruct((m, d), F32),
        grid=(m // POOL_BT,),
        in_specs=[pl.BlockSpec((POOL_BT, d), row),
                  pl.BlockSpec((HALO, d), lambda i: (jnp.maximum(i * rb - 1, 0), 0)),
                  pl.BlockSpec((HALO, d), lambda i: (jnp.minimum((i + 1) * rb, n_halo_blocks - 1), 0)),
                  pl.BlockSpec((None,) + w_all.shape[1:], lambda i: (layer, 0, 0, 0)),
                  _vec_spec(d, layer), _vec_spec(d, layer),
                  _vec_spec(d, norm_layer), _vec_spec(d, norm_layer)],
        out_specs=pl.BlockSpec((POOL_BT, d), row),
        scratch_shapes=[pltpu.VMEM((POOL_BT, d), F32)],
        compiler_params=pltpu.CompilerParams(
            dimension_semantics=("parallel",), vmem_limit_bytes=VMEM_LIMIT),
        name="pool_mixer_norm",
    )(x, x, x, w_all, bias_all, scale_all, g_all, b_all)


def _mlp_kernel(x_ref, w1_ref, w2_ref, g_ref, b_ref, o_ref, xb_ref):
    j = pl.program_id(1)

    @pl.when(j == 0)
    def _():
        xb_ref[...] = x_ref[...].astype(BF16)
        o_ref[...] = jnp.zeros_like(o_ref)

    hid = jnp.dot(xb_ref[...], w1_ref[...], preferred_element_type=F32)
    hid = jnp.square(jnp.maximum(hid, 0.0)).astype(BF16)
    o_ref[...] += jnp.dot(hid, w2_ref[...].astype(BF16), preferred_element_type=F32)

    @pl.when(j == pl.num_programs(1) - 1)
    def _():
        _residual_norm_inplace(x_ref, g_ref, b_ref, o_ref)


def _mlp_norm(x, w1_blocked, w2_all, g_all, b_all, layer):
    m, d = x.shape
    f = w2_all.shape[1]
    row = lambda i, j: (i, 0)
    return pl.pallas_call(
        _mlp_kernel,
        out_shape=jax.ShapeDtypeStruct((m, d), F32),
        grid=(m // MLP_BM, f // MLP_BF),
        in_specs=[pl.BlockSpec((MLP_BM, d), row),
                  pl.BlockSpec((None, None, d, MLP_BF), lambda i, j: (layer, j, 0, 0)),
                  pl.BlockSpec((None, MLP_BF, d), lambda i, j: (layer, j, 0)),
                  _vec_spec(d, layer), _vec_spec(d, layer)],
        out_specs=pl.BlockSpec((MLP_BM, d), row),
        scratch_shapes=[pltpu.VMEM((MLP_BM, d), BF16)],
        compiler_params=pltpu.CompilerParams(
            dimension_semantics=("parallel", "arbitrary"), vmem_limit_bytes=VMEM_LIMIT),
        name="mlp_norm",
    )(x, w1_blocked, w2_all, g_all, b_all)


def kernel(x, ssd_in_proj, ssd_conv_w, ssd_conv_b, ssd_dt_bias, ssd_A_log, ssd_D, ssd_norm_w,
           ssd_out_proj, pool_w, pool_b, pool_scale, mlp_w1, mlp_w2, ln_mix_g, ln_mix_b,
           ln_ffn_g, ln_ffn_b):
    batch, seq, d = x.shape
    m = batch * seq
    assert d == D_MODEL and seq % (SSD_CPS * CHUNK) == 0 and seq % POOL_BT == 0 and seq % INPROJ_BM == 0
    n_ssd, n_pool = ssd_in_proj.shape[0], pool_w.shape[0]
    vecs = lambda v: v.reshape(v.shape[0], 1, -1)
    w_dt_t = jnp.swapaxes(ssd_in_proj[:, :, D_INNER + D_XBC:], 1, 2)
    w_out = ssd_out_proj.astype(BF16)
    w1_blocked = jnp.transpose(mlp_w1.astype(BF16).reshape(DEPTH, d, D_FF // MLP_BF, MLP_BF), (0, 2, 1, 3))
    conv_w = ssd_conv_w.reshape(n_ssd, CONV_WIDTH, D_XBC)
    conv_b = vecs(ssd_conv_b)
    head_shape = (n_ssd, 2, N_GROUPS, HEADS_PER_GROUP, 1)
    dt_bias = ssd_dt_bias.reshape(head_shape)
    a_log = ssd_A_log.reshape(head_shape)
    d_skip = vecs(jnp.repeat(ssd_D, HEAD_DIM, axis=-1))
    norm_w = vecs(ssd_norm_w)
    pool_b, pool_scale = pool_b.reshape(n_pool, 1, d), vecs(pool_scale)
    ln_mix_g, ln_mix_b, ln_ffn_g, ln_ffn_b = (vecs(v) for v in (ln_mix_g, ln_mix_b, ln_ffn_g, ln_ffn_b))

    xf = x.reshape(m, d)
    for i in range(DEPTH):
        j = i // 2
        if i % 2 == 0:
            xb = xf.astype(BF16)
            z = _in_proj(xb, ssd_in_proj, j, 0, D_INNER)
            xs = _in_proj_conv(xb, ssd_in_proj, conv_w, conv_b, j, seq, 0, D_INNER, F32)
            bc = _in_proj_conv(xb, ssd_in_proj, conv_w, conv_b, j, seq, D_INNER, 2 * D_BC, BF16)
            dt_t = _dt_proj(xb, w_dt_t, j, batch, seq).reshape(batch, 2, N_GROUPS, HEADS_PER_GROUP, seq)
            gy = _ssd_scan(xs, bc, z, dt_t, dt_bias, a_log, d_skip, norm_w, j, batch, seq)
            xf = _out_proj_norm(gy, w_out, xf, ln_mix_g, ln_mix_b, j, i)
        else:
            xf = _pool_mixer_norm(xf, pool_w, pool_b, pool_scale, ln_mix_g, ln_mix_b, j, i, seq)
        xf = _mlp_norm(xf, w1_blocked, mlp_w2, ln_ffn_g, ln_ffn_b, i)
    return xf.reshape(batch, seq, d)
```

```python
import functools

import jax
import jax.numpy as jnp
from jax import lax
from jax.experimental import pallas as pl
from jax.experimental.pallas import tpu as pltpu

F32 = jnp.float32
BF16 = jnp.bfloat16

D_MODEL = 2048
DEPTH = 4
D_INNER = 2 * D_MODEL
HEAD_DIM = 64
N_HEADS = D_INNER // HEAD_DIM
N_GROUPS = 8
HEADS_PER_GROUP = N_HEADS // N_GROUPS
GROUP_DIM = D_INNER // N_GROUPS
D_STATE = 128
CONV_WIDTH = 5
CHUNK = 128
D_BC = N_GROUPS * D_STATE
D_XBC = D_INNER + 2 * D_BC
D_IN_PROJ = D_INNER + D_XBC + 2 * N_HEADS
POOL_WINDOWS = (2, 4, 8, 16)
POOL_GROUP_DIM = D_MODEL // len(POOL_WINDOWS)
D_FF = 4 * D_MODEL
DEEPNORM_ALPHA = (2.0 * DEPTH) ** 0.25
LN_EPS = 1e-5
RMS_EPS = 1e-5
LOG2_E = 1.4426950408889634

SUBLANES = 8
LANES = 128
VMEM_LIMIT = 60 * 1024 * 1024

MLP_BM = 1024
MLP_BF = 512
INPROJ_BM = 1024
INPROJ_BN = 1024
OUTPROJ_BM = 1024
OUTPROJ_BK = 1024
CONV_BC = 256
CONV_BR = 128
POOL_BT = 512
SSD_CPS = 8
HALO = SUBLANES
HALO_BF16 = 2 * SUBLANES
SINGLE_BUFFER = pl.Buffered(1)


def _layer_norm(v, g, b):
    mu = jnp.mean(v, axis=-1, keepdims=True)
    d = v - mu
    var = jnp.mean(d * d, axis=-1, keepdims=True)
    return d * lax.rsqrt(var + LN_EPS) * g + b


def _silu(v):
    return v * (1.0 / (1.0 + jnp.exp2(v * -LOG2_E)))


def _split_bf16(v, parts):
    out = []
    r = v
    for _ in range(parts):
        p = r.astype(BF16)
        out.append(p)
        r = r - p.astype(F32)
    return out


def _store_group_major(o_ref, rows, col0, val):
    gw = o_ref.shape[2]
    step = min(gw, val.shape[1])
    for off in range(0, val.shape[1], step):
        c = col0 + off
        o_ref[c // gw, rows, c % gw:c % gw + step] = val[:, off:off + step].astype(o_ref.dtype)


def _group_major_out(m, n_cols, group_width, dtype, index_map):
    shape = jax.ShapeDtypeStruct((n_cols // group_width, m, group_width), dtype)
    spec = pl.BlockSpec((INPROJ_BN // group_width, INPROJ_BM, group_width), index_map)
    return shape, spec


def _matmul_kernel(x_ref, w_ref, o_ref):
    res = jnp.dot(x_ref[...], w_ref[...].astype(BF16), preferred_element_type=F32)
    _store_group_major(o_ref, slice(None), 0, res)


def _in_proj(xb, w_all, layer, col_start, n_cols, group_width):
    m, k = xb.shape
    c0 = col_start // INPROJ_BN
    out_shape, out_spec = _group_major_out(m, n_cols, group_width, F32, lambda j, i: (j, i, 0))
    return pl.pallas_call(
        _matmul_kernel,
        out_shape=out_shape,
        grid=(n_cols // INPROJ_BN, m // INPROJ_BM),
        in_specs=[pl.BlockSpec((INPROJ_BM, k), lambda j, i: (i, 0)),
                  pl.BlockSpec((None, k, INPROJ_BN), lambda j, i: (layer, 0, j + c0))],
        out_specs=out_spec,
        compiler_params=pltpu.CompilerParams(
            dimension_semantics=("parallel", "parallel"), vmem_limit_bytes=VMEM_LIMIT),
        name="in_proj",
    )(xb, w_all)


def _dt_proj_kernel(w_ref, x_ref, o_ref):
    o_ref[0] = lax.dot_general(w_ref[...].astype(BF16), x_ref[...], (((1,), (1,)), ((), ())),
                               preferred_element_type=F32)


def _dt_proj(xb, w_dt_t_all, layer, batch, seq):
    m, k = xb.shape
    nh = w_dt_t_all.shape[1]
    seq_tiles = seq // INPROJ_BM
    return pl.pallas_call(
        _dt_proj_kernel,
        out_shape=jax.ShapeDtypeStruct((batch, nh, seq), F32),
        grid=(m // INPROJ_BM,),
        in_specs=[pl.BlockSpec((None, nh, k), lambda i: (layer, 0, 0)),
                  pl.BlockSpec((INPROJ_BM, k), lambda i: (i, 0))],
        out_specs=pl.BlockSpec((1, nh, INPROJ_BM), lambda i: (i // seq_tiles, 0, i % seq_tiles)),
        compiler_params=pltpu.CompilerParams(
            dimension_semantics=("parallel",), vmem_limit_bytes=VMEM_LIMIT),
        name="dt_proj",
    )(w_dt_t_all, xb)


def _in_proj_conv_kernel(seq_tiles, xm_ref, xp_ref, xn_ref, w_ref, cw_ref, cb_ref, o_ref, xext_ref):
    bt = xm_ref.shape[0]
    hb = HALO_BF16

    @pl.when(pl.program_id(1) == 0)
    def _():
        pos = pl.program_id(0) % seq_tiles
        xext_ref[0:hb, :] = jnp.where(pos == 0, jnp.zeros_like(xp_ref), xp_ref[...])
        xext_ref[hb:hb + bt, :] = xm_ref[...]
        xext_ref[hb + bt:2 * hb + bt, :] = jnp.where(pos == seq_tiles - 1, jnp.zeros_like(xn_ref),
                                                     xn_ref[...])

    pad = CONV_WIDTH // 2
    rb, halo = CONV_BR, HALO
    for c in range(w_ref.shape[1] // CONV_BC):
        cols = slice(c * CONV_BC, (c + 1) * CONV_BC)
        pre = jnp.dot(xext_ref[...], w_ref[:, cols].astype(BF16), preferred_element_type=F32)
        for r in range(0, bt, rb):
            blk = pre[hb + r - halo:hb + r + rb + halo]
            acc = cb_ref[:, cols] + cw_ref[pad:pad + 1, cols] * blk[halo:halo + rb]
            for k in range(CONV_WIDTH):
                if k != pad:
                    shifted = pltpu.roll(blk, (pad - k) % (rb + 2 * halo), axis=0)[halo:halo + rb]
                    acc = acc + cw_ref[k:k + 1, cols] * shifted
            _store_group_major(o_ref, slice(r, r + rb), c * CONV_BC, _silu(acc))


def _in_proj_conv(xb, w_all, conv_w_all, conv_b_all, layer, seq, col_start, n_cols, group_width,
                  out_dtype):
    m, k = xb.shape
    w0 = (D_INNER + col_start) // INPROJ_BN
    c0 = col_start // INPROJ_BN
    rb = INPROJ_BM // HALO_BF16
    n_halo_blocks = m // HALO_BF16
    out_shape, out_spec = _group_major_out(m, n_cols, group_width, out_dtype, lambda i, j: (j, i, 0))
    return pl.pallas_call(
        functools.partial(_in_proj_conv_kernel, seq // INPROJ_BM),
        out_shape=out_shape,
        grid=(m // INPROJ_BM, n_cols // INPROJ_BN),
        in_specs=[
            pl.BlockSpec((INPROJ_BM, k), lambda i, j: (i, 0)),
            pl.BlockSpec((HALO_BF16, k), lambda i, j: (jnp.maximum(i * rb - 1, 0), 0)),
            pl.BlockSpec((HALO_BF16, k), lambda i, j: (jnp.minimum((i + 1) * rb, n_halo_blocks - 1), 0)),
            pl.BlockSpec((None, k, INPROJ_BN), lambda i, j: (layer, 0, j + w0)),
            pl.BlockSpec((None, CONV_WIDTH, INPROJ_BN), lambda i, j: (layer, 0, j + c0)),
            pl.BlockSpec((None, 1, INPROJ_BN), lambda i, j: (layer, 0, j + c0)),
        ],
        out_specs=out_spec,
        scratch_shapes=[pltpu.VMEM((INPROJ_BM + 2 * HALO_BF16, k), BF16)],
        compiler_params=pltpu.CompilerParams(
            dimension_semantics=("parallel", "arbitrary"), vmem_limit_bytes=VMEM_LIMIT),
        name="in_proj_conv",
    )(xb, xb, xb, w_all, conv_w_all, conv_b_all)


ROW_CS, ROW_WEND, ROW_WIN = (HEADS_PER_GROUP * r for r in range(3))
TABLE_ROWS = 3 * HEADS_PER_GROUP


def _ssd_block(rev, x_ref, b_ref, c_ref, z_ref, dt_ref, bias_ref, alog_ref, dskip_ref, nw_ref,
               tri_ref, negmask_ref, ef_ref, o_ref, h_ref, yf_ref):
    hpg = HEADS_PER_GROUP
    cl = SSD_CPS * CHUNK
    cb = pl.program_id(3)
    blk = pl.num_programs(3) - 1 - cb if rev else cb
    g0 = pl.multiple_of(blk * cl, cl)
    tri = tri_ref[int(rev)]
    negmask = negmask_ref[int(rev)]

    raw = dt_ref[0, 0, 0] + bias_ref[0, 0]
    dt = jnp.maximum(raw, 0.0) + jnp.log1p(jnp.exp(-jnp.abs(raw)))
    a = dt * (-LOG2_E * jnp.exp(alog_ref[0, 0]))
    log_dt = jnp.log2(dt)
    zero8 = jnp.zeros((hpg, CHUNK), F32)
    parts = []
    for k in range(SSD_CPS):
        parts += [p.astype(F32) for p in _split_bf16(a[:, k * CHUNK:(k + 1) * CHUNK], 3)] + [zero8]
    cs3 = jnp.dot(jnp.concatenate(parts, axis=0).astype(BF16), tri, preferred_element_type=F32)
    pad = jnp.zeros((CHUNK - TABLE_ROWS, CHUNK), F32)
    row_sub, tab_t = [], []
    for k in range(SSD_CPS):
        c3 = cs3[4 * hpg * k:4 * hpg * (k + 1)]
        cs = c3[0:hpg] + c3[hpg:2 * hpg] + c3[2 * hpg:3 * hpg]
        tot = cs[:, 0:1] if rev else cs[:, CHUNK - 1:CHUNK]
        w_end = jnp.exp2(tot - cs) * dt[:, k * CHUNK:(k + 1) * CHUNK]
        w_in = jnp.exp2(cs)
        tab_t.append(jnp.concatenate([cs, w_end, w_in, pad], axis=0).T)
        row_sub.append(cs - log_dt[:, k * CHUNK:(k + 1) * CHUNK])
    tab_t = jnp.concatenate(tab_t, axis=0)
    t_hi, t_lo = _split_bf16(tab_t, 2)
    fe = jnp.dot(jnp.concatenate([t_hi, t_lo], axis=1), ef_ref[...], preferred_element_type=F32)

    lane = lax.broadcasted_iota(jnp.int32, (CHUNK, GROUP_DIM), 1)
    even_head = (lane % (2 * HEAD_DIM)) < HEAD_DIM
    h = h_ref[...]
    for k in (reversed(range(SSD_CPS)) if rev else range(SSD_CPS)):
        rows = slice(k * CHUNK, (k + 1) * CHUNK)
        x = x_ref[rows, :]
        bmat = b_ref[rows, :]
        cmat = c_ref[rows, :]
        wend_e = fe[rows, 0:GROUP_DIM]
        win_e = fe[rows, GROUP_DIM:2 * GROUP_DIM]
        scores = lax.dot_general(cmat, bmat, (((1,), (1,)), ((), ())),
                                 preferred_element_type=F32)
        x_even = jnp.where(even_head, x, 0.0).astype(BF16)
        x_odd = jnp.where(even_head, 0.0, x).astype(BF16)
        tab_k = tab_t[rows]
        yd = []
        for j in range(hpg // 2):
            m_pair = []
            for hh in (2 * j, 2 * j + 1):
                seg = (jnp.broadcast_to(tab_k[:, ROW_CS + hh:ROW_CS + hh + 1], (CHUNK, CHUNK))
                       - row_sub[k][hh:hh + 1, :])
                m_pair.append((scores * jnp.exp2(seg + negmask)).astype(BF16))
            cols = slice(2 * j * HEAD_DIM, 2 * (j + 1) * HEAD_DIM)
            rhs = jnp.concatenate([x_even[:, cols], x_odd[:, cols]], axis=0)
            yd.append(jnp.dot(jnp.concatenate(m_pair, axis=1), rhs, preferred_element_type=F32))
        y = (jnp.concatenate(yd, axis=1)
             + jnp.dot(cmat, h.astype(BF16), preferred_element_type=F32) * win_e)
        st = lax.dot_general(bmat, (x * wend_e).astype(BF16), (((0,), (0,)), ((), ())),
                             preferred_element_type=F32)
        chunk_decay = win_e[0:1, :] if rev else win_e[CHUNK - 1:CHUNK, :]
        h = h * chunk_decay + st

        grows = pl.ds(g0 + k * CHUNK, CHUNK)
        if not rev:
            yf_ref[grows, :] = y
        else:
            ytot = yf_ref[grows, :] + y + x * dskip_ref[...]
            gy = ytot * _silu(z_ref[rows, :])
            ms = jnp.mean(gy * gy, axis=-1, keepdims=True)
            o_ref[rows, :] = (gy * lax.rsqrt(ms + RMS_EPS) * nw_ref[...]).astype(o_ref.dtype)
    h_ref[...] = h


def _ssd_kernel(*refs):
    h_ref = refs[-2]

    @pl.when(pl.program_id(3) == 0)
    def _():
        h_ref[...] = jnp.zeros_like(h_ref)

    @pl.when(pl.program_id(2) == 0)
    def _():
        _ssd_block(False, *refs)

    @pl.when(pl.program_id(2) == 1)
    def _():
        _ssd_block(True, *refs)


def _ssd_constants():
    r = lax.broadcasted_iota(jnp.int32, (CHUNK, CHUNK), 0)
    c = lax.broadcasted_iota(jnp.int32, (CHUNK, CHUNK), 1)
    feeds = jnp.stack([c <= r, c >= r])
    negmask = jnp.where(feeds, 0.0, -jnp.inf).astype(F32)
    tri = jnp.swapaxes(feeds, 1, 2).astype(BF16)
    j = lax.broadcasted_iota(jnp.int32, (CHUNK, 2 * GROUP_DIM), 0)
    n = lax.broadcasted_iota(jnp.int32, (CHUNK, 2 * GROUP_DIM), 1)
    ef = (j == ROW_WEND + HEADS_PER_GROUP * (n // GROUP_DIM) + (n % GROUP_DIM) // HEAD_DIM).astype(BF16)
    return tri, negmask, jnp.concatenate([ef, ef], axis=0)


def _ssd_scan(xs, bc, z, dt_t, dt_bias_all, a_log_all, d_skip_all, norm_w_all, layer, batch, seq):
    m = xs.shape[1]
    cl = SSD_CPS * CHUNK
    n_cb = seq // cl
    tri, negmask, ef = _ssd_constants()

    def in_blk(b, ph, c):
        return b * n_cb + c + ph * (n_cb - 1 - 2 * c)

    def out_blk(b, ph, c):
        return b * n_cb + n_cb - 1 - ph * c

    head_spec = pl.BlockSpec((None, 1, 1, HEADS_PER_GROUP, 1), lambda b, g, ph, c: (layer, ph, g, 0, 0))
    chan_spec = pl.BlockSpec((None, 1, GROUP_DIM), lambda b, g, ph, c: (layer, 0, g))
    return pl.pallas_call(
        _ssd_kernel,
        out_shape=jax.ShapeDtypeStruct((N_GROUPS, m, GROUP_DIM), BF16),
        grid=(batch, N_GROUPS, 2, n_cb),
        in_specs=[
            pl.BlockSpec((None, cl, GROUP_DIM), lambda b, g, ph, c: (g, in_blk(b, ph, c), 0)),
            pl.BlockSpec((None, cl, D_STATE), lambda b, g, ph, c: (g, in_blk(b, ph, c), 0)),
            pl.BlockSpec((None, cl, D_STATE), lambda b, g, ph, c: (N_GROUPS + g, in_blk(b, ph, c), 0)),
            pl.BlockSpec((None, cl, GROUP_DIM), lambda b, g, ph, c: (g, out_blk(b, ph, c), 0)),
            pl.BlockSpec((1, 1, 1, HEADS_PER_GROUP, cl),
                         lambda b, g, ph, c: (b, ph, g, 0, c + ph * (n_cb - 1 - 2 * c))),
            head_spec, head_spec, chan_spec, chan_spec,
            pl.BlockSpec(tri.shape, lambda b, g, ph, c: (0, 0, 0)),
            pl.BlockSpec(negmask.shape, lambda b, g, ph, c: (0, 0, 0)),
            pl.BlockSpec(ef.shape, lambda b, g, ph, c: (0, 0)),
        ],
        out_specs=pl.BlockSpec((None, cl, GROUP_DIM), lambda b, g, ph, c: (g, out_blk(b, ph, c), 0)),
        scratch_shapes=[pltpu.VMEM((D_STATE, GROUP_DIM), F32),
                        pltpu.VMEM((seq, GROUP_DIM), F32)],
        compiler_params=pltpu.CompilerParams(
            dimension_semantics=("arbitrary", "arbitrary", "arbitrary", "arbitrary"),
            vmem_limit_bytes=VMEM_LIMIT),
        name="ssd_scan",
    )(xs, bc, bc, z, dt_t, dt_bias_all, a_log_all, d_skip_all, norm_w_all, tri, negmask, ef)


def _vec_spec(n, layer):
    return pl.BlockSpec((None, 1, n), lambda *_: (layer, 0, 0))


def _residual_norm_inplace(x_ref, g_ref, b_ref, o_ref):
    o_ref[...] = _layer_norm(DEEPNORM_ALPHA * x_ref[...] + o_ref[...], g_ref[...], b_ref[...])


def _outproj_kernel(a_ref, w_ref, x_ref, g_ref, b_ref, o_ref):
    j = pl.program_id(1)

    @pl.when(j == 0)
    def _():
        o_ref[...] = jnp.zeros_like(o_ref)

    a = jnp.concatenate([a_ref[g] for g in range(a_ref.shape[0])], axis=1)
    o_ref[...] += jnp.dot(a, w_ref[...], preferred_element_type=F32)

    @pl.when(j == pl.num_programs(1) - 1)
    def _():
        _residual_norm_inplace(x_ref, g_ref, b_ref, o_ref)


def _out_proj_norm(gy, w_all, x, g_all, b_all, layer, norm_layer):
    n_groups, m, gd = gy.shape
    k = n_groups * gd
    n = w_all.shape[2]
    row = lambda i, j: (i, 0)
    return pl.pallas_call(
        _outproj_kernel,
        out_shape=jax.ShapeDtypeStruct((m, n), F32),
        grid=(m // OUTPROJ_BM, k // OUTPROJ_BK),
        in_specs=[pl.BlockSpec((OUTPROJ_BK // gd, OUTPROJ_BM, gd), lambda i, j: (j, i, 0)),
                  pl.BlockSpec((None, OUTPROJ_BK, n), lambda i, j: (layer, j, 0)),
                  pl.BlockSpec((OUTPROJ_BM, n), row),
                  _vec_spec(n, norm_layer), _vec_spec(n, norm_layer)],
        out_specs=pl.BlockSpec((OUTPROJ_BM, n), row),
        compiler_params=pltpu.CompilerParams(
            dimension_semantics=("parallel", "arbitrary"), vmem_limit_bytes=VMEM_LIMIT),
        name="out_proj_norm",
    )(gy, w_all, x, g_all, b_all)


def _pool_kernel(seq, x_ref, p_ref, n_ref, w_ref, bias_ref, scale_ref, g_ref, b_ref,
                 o_ref, v_ref):
    bt = x_ref.shape[0]
    n_ext = bt + 2 * HALO
    seq_tiles = seq // bt
    pos = pl.program_id(0) % seq_tiles
    t = pos * bt + lax.broadcasted_iota(jnp.int32, (bt, 1), 0)
    gd = POOL_GROUP_DIM
    shift_down = lambda v, s: pltpu.roll(v, s % n_ext, axis=0)
    for gi, win in enumerate(POOL_WINDOWS):
        cols = slice(gi * gd, (gi + 1) * gd)
        start = t - win // 2
        cnt = (jnp.clip(start + win, 0, seq) - jnp.clip(start, 0, seq)).astype(F32)
        ext = jnp.concatenate([jnp.where(pos == 0, 0.0, p_ref[:, cols]), x_ref[:, cols],
                               jnp.where(pos == seq_tiles - 1, 0.0, n_ref[:, cols])], axis=0)
        psum, span = ext, 1
        while span < win:
            psum = psum + shift_down(psum, span)
            span *= 2
        ahead = win - win // 2 - 1
        wsum = (shift_down(psum, -ahead) if ahead else psum)[HALO:HALO + bt]
        xg = x_ref[:, cols]
        mg = (wsum / cnt - xg).astype(BF16)
        y = jnp.dot(mg, w_ref[gi].astype(BF16), preferred_element_type=F32) + bias_ref[:, cols]
        v_ref[:, cols] = DEEPNORM_ALPHA * xg + y * scale_ref[:, cols]
    o_ref[...] = _layer_norm(v_ref[...], g_ref[...], b_ref[...])


def _pool_mixer_norm(x, w_all, bias_all, scale_all, g_all, b_all, layer, norm_layer, seq):
    m, d = x.shape
    rb = POOL_BT // HALO
    n_halo_blocks = m // HALO
    row = lambda i: (i, 0)
    return pl.pallas_call(
        functools.partial(_pool_kernel, seq),
        out_shape=jax.ShapeDtypeStruct((m, d), F32),
        grid=(m // POOL_BT,),
        in_specs=[pl.BlockSpec((POOL_BT, d), row),
                  pl.BlockSpec((HALO, d), lambda i: (jnp.maximum(i * rb - 1, 0), 0)),
                  pl.BlockSpec((HALO, d), lambda i: (jnp.minimum((i + 1) * rb, n_halo_blocks - 1), 0)),
                  pl.BlockSpec((None,) + w_all.shape[1:], lambda i: (layer, 0, 0, 0)),
                  _vec_spec(d, layer), _vec_spec(d, layer),
                  _vec_spec(d, norm_layer), _vec_spec(d, norm_layer)],
        out_specs=pl.BlockSpec((POOL_BT, d), row),
        scratch_shapes=[pltpu.VMEM((POOL_BT, d), F32)],
        compiler_params=pltpu.CompilerParams(
            dimension_semantics=("parallel",), vmem_limit_bytes=VMEM_LIMIT),
        name="pool_mixer_norm",
    )(x, x, x, w_all, bias_all, scale_all, g_all, b_all)


def _mlp_kernel(x_ref, w1_ref, w2_ref, g_ref, b_ref, o_ref, xb_ref):
    j = pl.program_id(1)

    @pl.when(j == 0)
    def _():
        xb_ref[...] = x_ref[...].astype(BF16)
        o_ref[...] = jnp.zeros_like(o_ref)

    hid = jnp.dot(xb_ref[...], w1_ref[...], preferred_element_type=F32)
    hid = jnp.square(jnp.maximum(hid, 0.0)).astype(BF16)
    o_ref[...] += jnp.dot(hid, w2_ref[...].astype(BF16), preferred_element_type=F32)

    @pl.when(j == pl.num_programs(1) - 1)
    def _():
        _residual_norm_inplace(x_ref, g_ref, b_ref, o_ref)


def _mlp_norm(x, w1_all, w2_all, g_all, b_all, layer):
    m, d = x.shape
    f = w2_all.shape[1]
    row = lambda i, j: (i, 0)
    return pl.pallas_call(
        _mlp_kernel,
        out_shape=jax.ShapeDtypeStruct((m, d), F32),
        grid=(m // MLP_BM, f // MLP_BF),
        in_specs=[pl.BlockSpec((MLP_BM, d), row),
                  pl.BlockSpec((None, d, MLP_BF), lambda i, j: (layer, 0, j)),
                  pl.BlockSpec((None, MLP_BF, d), lambda i, j: (layer, j, 0)),
                  _vec_spec(d, layer), _vec_spec(d, layer)],
        out_specs=pl.BlockSpec((MLP_BM, d), row),
        scratch_shapes=[pltpu.VMEM((MLP_BM, d), BF16)],
        compiler_params=pltpu.CompilerParams(
            dimension_semantics=("parallel", "arbitrary"), vmem_limit_bytes=VMEM_LIMIT),
        name="mlp_norm",
    )(x, w1_all, w2_all, g_all, b_all)


def kernel(x, ssd_in_proj, ssd_conv_w, ssd_conv_b, ssd_dt_bias, ssd_A_log, ssd_D, ssd_norm_w,
           ssd_out_proj, pool_w, pool_b, pool_scale, mlp_w1, mlp_w2, ln_mix_g, ln_mix_b,
           ln_ffn_g, ln_ffn_b):
    batch, seq, d = x.shape
    m = batch * seq
    assert d == D_MODEL and seq % (SSD_CPS * CHUNK) == 0 and seq % POOL_BT == 0 and seq % INPROJ_BM == 0
    n_ssd, n_pool = ssd_in_proj.shape[0], pool_w.shape[0]
    vecs = lambda v: v.reshape(v.shape[0], 1, -1)
    w_dt_t = jnp.swapaxes(ssd_in_proj[:, :, D_INNER + D_XBC:], 1, 2)
    w_out = ssd_out_proj.astype(BF16)
    w1 = mlp_w1.astype(BF16)
    conv_w = ssd_conv_w.reshape(n_ssd, CONV_WIDTH, D_XBC)
    conv_b = vecs(ssd_conv_b)
    head_shape = (n_ssd, 2, N_GROUPS, HEADS_PER_GROUP, 1)
    dt_bias = ssd_dt_bias.reshape(head_shape)
    a_log = ssd_A_log.reshape(head_shape)
    d_skip = vecs(jnp.repeat(ssd_D, HEAD_DIM, axis=-1))
    norm_w = vecs(ssd_norm_w)
    pool_b, pool_scale = pool_b.reshape(n_pool, 1, d), vecs(pool_scale)
    ln_mix_g, ln_mix_b, ln_ffn_g, ln_ffn_b = (vecs(v) for v in (ln_mix_g, ln_mix_b, ln_ffn_g, ln_ffn_b))

    xf = x.reshape(m, d)
    for i in range(DEPTH):
        j = i // 2
        if i % 2 == 0:
            xb = xf.astype(BF16)
            z = _in_proj(xb, ssd_in_proj, j, 0, D_INNER, GROUP_DIM)
            xs = _in_proj_conv(xb, ssd_in_proj, conv_w, conv_b, j, seq, 0, D_INNER, GROUP_DIM, F32)
            bc = _in_proj_conv(xb, ssd_in_proj, conv_w, conv_b, j, seq, D_INNER, 2 * D_BC, D_STATE, BF16)
            dt_t = _dt_proj(xb, w_dt_t, j, batch, seq).reshape(batch, 2, N_GROUPS, HEADS_PER_GROUP, seq)
            gy = _ssd_scan(xs, bc, z, dt_t, dt_bias, a_log, d_skip, norm_w, j, batch, seq)
            xf = _out_proj_norm(gy, w_out, xf, ln_mix_g, ln_mix_b, j, i)
        else:
            xf = _pool_mixer_norm(xf, pool_w, pool_b, pool_scale, ln_mix_g, ln_mix_b, j, i, seq)
        xf = _mlp_norm(xf, w1, mlp_w2, ln_ffn_g, ln_ffn_b, i)
    return xf.reshape(batch, seq, d)
```

```python
import functools

import jax
import jax.numpy as jnp
from jax import lax
from jax.experimental import pallas as pl
from jax.experimental.pallas import tpu as pltpu

F32 = jnp.float32
BF16 = jnp.bfloat16

D_MODEL = 2048
DEPTH = 4
D_INNER = 2 * D_MODEL
HEAD_DIM = 64
N_HEADS = D_INNER // HEAD_DIM
N_GROUPS = 8
HEADS_PER_GROUP = N_HEADS // N_GROUPS
GROUP_DIM = D_INNER // N_GROUPS
D_STATE = 128
CONV_WIDTH = 5
CHUNK = 128
D_BC = N_GROUPS * D_STATE
D_XBC = D_INNER + 2 * D_BC
D_IN_PROJ = D_INNER + D_XBC + 2 * N_HEADS
POOL_WINDOWS = (2, 4, 8, 16)
POOL_GROUP_DIM = D_MODEL // len(POOL_WINDOWS)
D_FF = 4 * D_MODEL
DEEPNORM_ALPHA = (2.0 * DEPTH) ** 0.25
LN_EPS = 1e-5
RMS_EPS = 1e-5
LOG2_E = 1.4426950408889634

SUBLANES = 8
LANES = 128
VMEM_LIMIT = 60 * 1024 * 1024

MLP_BM = 1024
MLP_BF = 512
INPROJ_BM = 1024
INPROJ_BN = 1024
OUTPROJ_BM = 1024
OUTPROJ_BK = 1024
CONV_BC = 256
CONV_BR = 128
POOL_BT = 512
SSD_CPS = 16
HALO = SUBLANES
HALO_BF16 = 2 * SUBLANES
SINGLE_BUFFER = pl.Buffered(1)


def _layer_norm(v, g, b):
    mu = jnp.mean(v, axis=-1, keepdims=True)
    d = v - mu
    var = jnp.mean(d * d, axis=-1, keepdims=True)
    return d * lax.rsqrt(var + LN_EPS) * g + b


def _silu(v):
    return v * (1.0 / (1.0 + jnp.exp2(v * -LOG2_E)))


def _split_bf16(v, parts):
    out = []
    r = v
    for _ in range(parts):
        p = r.astype(BF16)
        out.append(p)
        r = r - p.astype(F32)
    return out


def _store_group_major(o_ref, rows, col0, val):
    gw = o_ref.shape[2]
    step = min(gw, val.shape[1])
    for off in range(0, val.shape[1], step):
        c = col0 + off
        o_ref[c // gw, rows, c % gw:c % gw + step] = val[:, off:off + step].astype(o_ref.dtype)


def _group_major_out(m, n_cols, group_width, dtype, index_map):
    shape = jax.ShapeDtypeStruct((n_cols // group_width, m, group_width), dtype)
    spec = pl.BlockSpec((INPROJ_BN // group_width, INPROJ_BM, group_width), index_map)
    return shape, spec


def _matmul_kernel(x_ref, w_ref, o_ref):
    res = jnp.dot(x_ref[...], w_ref[...].astype(BF16), preferred_element_type=F32)
    _store_group_major(o_ref, slice(None), 0, res)


def _in_proj(xb, w_all, layer, col_start, n_cols, group_width):
    m, k = xb.shape
    c0 = col_start // INPROJ_BN
    out_shape, out_spec = _group_major_out(m, n_cols, group_width, F32, lambda j, i: (j, i, 0))
    return pl.pallas_call(
        _matmul_kernel,
        out_shape=out_shape,
        grid=(n_cols // INPROJ_BN, m // INPROJ_BM),
        in_specs=[pl.BlockSpec((INPROJ_BM, k), lambda j, i: (i, 0)),
                  pl.BlockSpec((None, k, INPROJ_BN), lambda j, i: (layer, 0, j + c0))],
        out_specs=out_spec,
        compiler_params=pltpu.CompilerParams(
            dimension_semantics=("parallel", "parallel"), vmem_limit_bytes=VMEM_LIMIT),
        name="in_proj",
    )(xb, w_all)


def _dt_proj_kernel(w_ref, x_ref, o_ref):
    o_ref[0] = lax.dot_general(w_ref[...].astype(BF16), x_ref[...], (((1,), (1,)), ((), ())),
                               preferred_element_type=F32)


def _dt_proj(xb, w_dt_t_all, layer, batch, seq):
    m, k = xb.shape
    nh = w_dt_t_all.shape[1]
    seq_tiles = seq // INPROJ_BM
    return pl.pallas_call(
        _dt_proj_kernel,
        out_shape=jax.ShapeDtypeStruct((batch, nh, seq), F32),
        grid=(m // INPROJ_BM,),
        in_specs=[pl.BlockSpec((None, nh, k), lambda i: (layer, 0, 0)),
                  pl.BlockSpec((INPROJ_BM, k), lambda i: (i, 0))],
        out_specs=pl.BlockSpec((1, nh, INPROJ_BM), lambda i: (i // seq_tiles, 0, i % seq_tiles)),
        compiler_params=pltpu.CompilerParams(
            dimension_semantics=("parallel",), vmem_limit_bytes=VMEM_LIMIT),
        name="dt_proj",
    )(w_dt_t_all, xb)


def _in_proj_conv_kernel(seq_tiles, xm_ref, xp_ref, xn_ref, w_ref, cw_ref, cb_ref, o_ref, xext_ref):
    bt = xm_ref.shape[0]
    hb = HALO_BF16

    @pl.when(pl.program_id(1) == 0)
    def _():
        pos = pl.program_id(0) % seq_tiles
        xext_ref[0:hb, :] = jnp.where(pos == 0, jnp.zeros_like(xp_ref), xp_ref[...])
        xext_ref[hb:hb + bt, :] = xm_ref[...]
        xext_ref[hb + bt:2 * hb + bt, :] = jnp.where(pos == seq_tiles - 1, jnp.zeros_like(xn_ref),
                                                     xn_ref[...])

    pad = CONV_WIDTH // 2
    rb, halo = CONV_BR, HALO
    for c in range(w_ref.shape[1] // CONV_BC):
        cols = slice(c * CONV_BC, (c + 1) * CONV_BC)
        pre = jnp.dot(xext_ref[...], w_ref[:, cols].astype(BF16), preferred_element_type=F32)
        for r in range(0, bt, rb):
            blk = pre[hb + r - halo:hb + r + rb + halo]
            acc = cb_ref[:, cols] + cw_ref[pad:pad + 1, cols] * blk[halo:halo + rb]
            for k in range(CONV_WIDTH):
                if k != pad:
                    shifted = pltpu.roll(blk, (pad - k) % (rb + 2 * halo), axis=0)[halo:halo + rb]
                    acc = acc + cw_ref[k:k + 1, cols] * shifted
            _store_group_major(o_ref, slice(r, r + rb), c * CONV_BC, _silu(acc))


def _in_proj_conv(xb, w_all, conv_w_all, conv_b_all, layer, seq, col_start, n_cols, group_width,
                  out_dtype):
    m, k = xb.shape
    w0 = (D_INNER + col_start) // INPROJ_BN
    c0 = col_start // INPROJ_BN
    rb = INPROJ_BM // HALO_BF16
    n_halo_blocks = m // HALO_BF16
    out_shape, out_spec = _group_major_out(m, n_cols, group_width, out_dtype, lambda i, j: (j, i, 0))
    return pl.pallas_call(
        functools.partial(_in_proj_conv_kernel, seq // INPROJ_BM),
        out_shape=out_shape,
        grid=(m // INPROJ_BM, n_cols // INPROJ_BN),
        in_specs=[
            pl.BlockSpec((INPROJ_BM, k), lambda i, j: (i, 0)),
            pl.BlockSpec((HALO_BF16, k), lambda i, j: (jnp.maximum(i * rb - 1, 0), 0)),
            pl.BlockSpec((HALO_BF16, k), lambda i, j: (jnp.minimum((i + 1) * rb, n_halo_blocks - 1), 0)),
            pl.BlockSpec((None, k, INPROJ_BN), lambda i, j: (layer, 0, j + w0)),
            pl.BlockSpec((None, CONV_WIDTH, INPROJ_BN), lambda i, j: (layer, 0, j + c0)),
            pl.BlockSpec((None, 1, INPROJ_BN), lambda i, j: (layer, 0, j + c0)),
        ],
        out_specs=out_spec,
        scratch_shapes=[pltpu.VMEM((INPROJ_BM + 2 * HALO_BF16, k), BF16)],
        compiler_params=pltpu.CompilerParams(
            dimension_semantics=("parallel", "arbitrary"), vmem_limit_bytes=VMEM_LIMIT),
        name="in_proj_conv",
    )(xb, xb, xb, w_all, conv_w_all, conv_b_all)


ROW_CS, ROW_WEND, ROW_WIN = (HEADS_PER_GROUP * r for r in range(3))
TABLE_ROWS = 3 * HEADS_PER_GROUP


def _ssd_block(rev, x_ref, b_ref, c_ref, z_ref, dt_ref, bias_ref, alog_ref, dskip_ref, nw_ref,
               tri_ref, negmask_ref, ef_ref, o_ref, h_ref, yf_ref):
    hpg = HEADS_PER_GROUP
    cl = SSD_CPS * CHUNK
    cb = pl.program_id(3)
    blk = pl.num_programs(3) - 1 - cb if rev else cb
    g0 = pl.multiple_of(blk * cl, cl)
    tri = tri_ref[int(rev)]
    negmask = negmask_ref[int(rev)]

    raw = dt_ref[0, 0, 0] + bias_ref[0, 0]
    dt = jnp.maximum(raw, 0.0) + jnp.log1p(jnp.exp(-jnp.abs(raw)))
    a = dt * (-LOG2_E * jnp.exp(alog_ref[0, 0]))
    log_dt = jnp.log2(dt)
    zero8 = jnp.zeros((hpg, CHUNK), F32)
    parts = []
    for k in range(SSD_CPS):
        parts += [p.astype(F32) for p in _split_bf16(a[:, k * CHUNK:(k + 1) * CHUNK], 3)] + [zero8]
    cs3 = jnp.dot(jnp.concatenate(parts, axis=0).astype(BF16), tri, preferred_element_type=F32)
    pad = jnp.zeros((CHUNK - TABLE_ROWS, CHUNK), F32)
    row_sub, tab_t = [], []
    for k in range(SSD_CPS):
        c3 = cs3[4 * hpg * k:4 * hpg * (k + 1)]
        cs = c3[0:hpg] + c3[hpg:2 * hpg] + c3[2 * hpg:3 * hpg]
        tot = cs[:, 0:1] if rev else cs[:, CHUNK - 1:CHUNK]
        w_end = jnp.exp2(tot - cs) * dt[:, k * CHUNK:(k + 1) * CHUNK]
        w_in = jnp.exp2(cs)
        tab_t.append(jnp.concatenate([cs, w_end, w_in, pad], axis=0).T)
        row_sub.append(cs - log_dt[:, k * CHUNK:(k + 1) * CHUNK])
    tab_t = jnp.concatenate(tab_t, axis=0)
    t_hi, t_lo = _split_bf16(tab_t, 2)
    fe = jnp.dot(jnp.concatenate([t_hi, t_lo], axis=1), ef_ref[...], preferred_element_type=F32)

    lane = lax.broadcasted_iota(jnp.int32, (CHUNK, GROUP_DIM), 1)
    even_head = (lane % (2 * HEAD_DIM)) < HEAD_DIM
    h = h_ref[...]
    for k in (reversed(range(SSD_CPS)) if rev else range(SSD_CPS)):
        rows = slice(k * CHUNK, (k + 1) * CHUNK)
        x = x_ref[rows, :]
        bmat = b_ref[rows, :]
        cmat = c_ref[rows, :]
        wend_e = fe[rows, 0:GROUP_DIM]
        win_e = fe[rows, GROUP_DIM:2 * GROUP_DIM]
        scores = lax.dot_general(cmat, bmat, (((1,), (1,)), ((), ())),
                                 preferred_element_type=F32)
        x_even = jnp.where(even_head, x, 0.0).astype(BF16)
        x_odd = jnp.where(even_head, 0.0, x).astype(BF16)
        tab_k = tab_t[rows]
        yd = []
        for j in range(hpg // 2):
            m_pair = []
            for hh in (2 * j, 2 * j + 1):
                seg = (jnp.broadcast_to(tab_k[:, ROW_CS + hh:ROW_CS + hh + 1], (CHUNK, CHUNK))
                       - row_sub[k][hh:hh + 1, :])
                m_pair.append((scores * jnp.exp2(seg + negmask)).astype(BF16))
            cols = slice(2 * j * HEAD_DIM, 2 * (j + 1) * HEAD_DIM)
            rhs = jnp.concatenate([x_even[:, cols], x_odd[:, cols]], axis=0)
            yd.append(jnp.dot(jnp.concatenate(m_pair, axis=1), rhs, preferred_element_type=F32))
        y = (jnp.concatenate(yd, axis=1)
             + jnp.dot(cmat, h.astype(BF16), preferred_element_type=F32) * win_e)
        st = lax.dot_general(bmat, (x * wend_e).astype(BF16), (((0,), (0,)), ((), ())),
                             preferred_element_type=F32)
        chunk_decay = win_e[0:1, :] if rev else win_e[CHUNK - 1:CHUNK, :]
        h = h * chunk_decay + st

        grows = pl.ds(g0 + k * CHUNK, CHUNK)
        if not rev:
            yf_ref[grows, :] = y
        else:
            ytot = yf_ref[grows, :] + y + x * dskip_ref[...]
            gy = ytot * _silu(z_ref[rows, :])
            ms = jnp.mean(gy * gy, axis=-1, keepdims=True)
            o_ref[rows, :] = (gy * lax.rsqrt(ms + RMS_EPS) * nw_ref[...]).astype(o_ref.dtype)
    h_ref[...] = h


def _ssd_kernel(*refs):
    h_ref = refs[-2]

    @pl.when(pl.program_id(3) == 0)
    def _():
        h_ref[...] = jnp.zeros_like(h_ref)

    @pl.when(pl.program_id(2) == 0)
    def _():
        _ssd_block(False, *refs)

    @pl.when(pl.program_id(2) == 1)
    def _():
        _ssd_block(True, *refs)


def _ssd_constants():
    r = lax.broadcasted_iota(jnp.int32, (CHUNK, CHUNK), 0)
    c = lax.broadcasted_iota(jnp.int32, (CHUNK, CHUNK), 1)
    feeds = jnp.stack([c <= r, c >= r])
    negmask = jnp.where(feeds, 0.0, -jnp.inf).astype(F32)
    tri = jnp.swapaxes(feeds, 1, 2).astype(BF16)
    j = lax.broadcasted_iota(jnp.int32, (CHUNK, 2 * GROUP_DIM), 0)
    n = lax.broadcasted_iota(jnp.int32, (CHUNK, 2 * GROUP_DIM), 1)
    ef = (j == ROW_WEND + HEADS_PER_GROUP * (n // GROUP_DIM) + (n % GROUP_DIM) // HEAD_DIM).astype(BF16)
    return tri, negmask, jnp.concatenate([ef, ef], axis=0)


def _ssd_scan(xs, bc, z, dt_t, dt_bias_all, a_log_all, d_skip_all, norm_w_all, layer, batch, seq):
    m = xs.shape[1]
    cl = SSD_CPS * CHUNK
    n_cb = seq // cl
    tri, negmask, ef = _ssd_constants()

    def in_blk(b, ph, c):
        return b * n_cb + c + ph * (n_cb - 1 - 2 * c)

    def out_blk(b, ph, c):
        return b * n_cb + n_cb - 1 - ph * c

    head_spec = pl.BlockSpec((None, 1, 1, HEADS_PER_GROUP, 1), lambda b, g, ph, c: (layer, ph, g, 0, 0))
    chan_spec = pl.BlockSpec((None, 1, GROUP_DIM), lambda b, g, ph, c: (layer, 0, g))
    return pl.pallas_call(
        _ssd_kernel,
        out_shape=jax.ShapeDtypeStruct((N_GROUPS, m, GROUP_DIM), BF16),
        grid=(batch, N_GROUPS, 2, n_cb),
        in_specs=[
            pl.BlockSpec((None, cl, GROUP_DIM), lambda b, g, ph, c: (g, in_blk(b, ph, c), 0)),
            pl.BlockSpec((None, cl, D_STATE), lambda b, g, ph, c: (g, in_blk(b, ph, c), 0)),
            pl.BlockSpec((None, cl, D_STATE), lambda b, g, ph, c: (N_GROUPS + g, in_blk(b, ph, c), 0)),
            pl.BlockSpec((None, cl, GROUP_DIM), lambda b, g, ph, c: (g, out_blk(b, ph, c), 0)),
            pl.BlockSpec((1, 1, 1, HEADS_PER_GROUP, cl),
                         lambda b, g, ph, c: (b, ph, g, 0, c + ph * (n_cb - 1 - 2 * c))),
            head_spec, head_spec, chan_spec, chan_spec,
            pl.BlockSpec(tri.shape, lambda b, g, ph, c: (0, 0, 0)),
            pl.BlockSpec(negmask.shape, lambda b, g, ph, c: (0, 0, 0)),
            pl.BlockSpec(ef.shape, lambda b, g, ph, c: (0, 0)),
        ],
        out_specs=pl.BlockSpec((None, cl, GROUP_DIM), lambda b, g, ph, c: (g, out_blk(b, ph, c), 0)),
        scratch_shapes=[pltpu.VMEM((D_STATE, GROUP_DIM), F32),
                        pltpu.VMEM((seq, GROUP_DIM), F32)],
        compiler_params=pltpu.CompilerParams(
            dimension_semantics=("arbitrary", "arbitrary", "arbitrary", "arbitrary"),
            vmem_limit_bytes=VMEM_LIMIT),
        name="ssd_scan",
    )(xs, bc, bc, z, dt_t, dt_bias_all, a_log_all, d_skip_all, norm_w_all, tri, negmask, ef)


def _vec_spec(n, layer):
    return pl.BlockSpec((None, 1, n), lambda *_: (layer, 0, 0))


def _residual_norm_inplace(x_ref, g_ref, b_ref, o_ref):
    o_ref[...] = _layer_norm(DEEPNORM_ALPHA * x_ref[...] + o_ref[...], g_ref[...], b_ref[...])


def _outproj_kernel(a_ref, w_ref, x_ref, g_ref, b_ref, o_ref):
    j = pl.program_id(1)

    @pl.when(j == 0)
    def _():
        o_ref[...] = jnp.zeros_like(o_ref)

    a = jnp.concatenate([a_ref[g] for g in range(a_ref.shape[0])], axis=1)
    o_ref[...] += jnp.dot(a, w_ref[...], preferred_element_type=F32)

    @pl.when(j == pl.num_programs(1) - 1)
    def _():
        _residual_norm_inplace(x_ref, g_ref, b_ref, o_ref)


def _out_proj_norm(gy, w_all, x, g_all, b_all, layer, norm_layer):
    n_groups, m, gd = gy.shape
    k = n_groups * gd
    n = w_all.shape[2]
    row = lambda i, j: (i, 0)
    return pl.pallas_call(
        _outproj_kernel,
        out_shape=jax.ShapeDtypeStruct((m, n), F32),
        grid=(m // OUTPROJ_BM, k // OUTPROJ_BK),
        in_specs=[pl.BlockSpec((OUTPROJ_BK // gd, OUTPROJ_BM, gd), lambda i, j: (j, i, 0)),
                  pl.BlockSpec((None, OUTPROJ_BK, n), lambda i, j: (layer, j, 0)),
                  pl.BlockSpec((OUTPROJ_BM, n), row),
                  _vec_spec(n, norm_layer), _vec_spec(n, norm_layer)],
        out_specs=pl.BlockSpec((OUTPROJ_BM, n), row),
        compiler_params=pltpu.CompilerParams(
            dimension_semantics=("parallel", "arbitrary"), vmem_limit_bytes=VMEM_LIMIT),
        name="out_proj_norm",
    )(gy, w_all, x, g_all, b_all)


def _pool_kernel(seq, x_ref, p_ref, n_ref, w_ref, bias_ref, scale_ref, g_ref, b_ref,
                 o_ref, v_ref):
    bt = x_ref.shape[0]
    n_ext = bt + 2 * HALO
    seq_tiles = seq // bt
    pos = pl.program_id(0) % seq_tiles
    t = pos * bt + lax.broadcasted_iota(jnp.int32, (bt, 1), 0)
    gd = POOL_GROUP_DIM
    shift_down = lambda v, s: pltpu.roll(v, s % n_ext, axis=0)
    for gi, win in enumerate(POOL_WINDOWS):
        cols = slice(gi * gd, (gi + 1) * gd)
        start = t - win // 2
        cnt = (jnp.clip(start + win, 0, seq) - jnp.clip(start, 0, seq)).astype(F32)
        ext = jnp.concatenate([jnp.where(pos == 0, 0.0, p_ref[:, cols]), x_ref[:, cols],
                               jnp.where(pos == seq_tiles - 1, 0.0, n_ref[:, cols])], axis=0)
        psum, span = ext, 1
        while span < win:
            psum = psum + shift_down(psum, span)
            span *= 2
        ahead = win - win // 2 - 1
        wsum = (shift_down(psum, -ahead) if ahead else psum)[HALO:HALO + bt]
        xg = x_ref[:, cols]
        mg = (wsum / cnt - xg).astype(BF16)
        y = jnp.dot(mg, w_ref[gi].astype(BF16), preferred_element_type=F32) + bias_ref[:, cols]
        v_ref[:, cols] = DEEPNORM_ALPHA * xg + y * scale_ref[:, cols]
    o_ref[...] = _layer_norm(v_ref[...], g_ref[...], b_ref[...])


def _pool_mixer_norm(x, w_all, bias_all, scale_all, g_all, b_all, layer, norm_layer, seq):
    m, d = x.shape
    rb = POOL_BT // HALO
    n_halo_blocks = m // HALO
    row = lambda i: (i, 0)
    return pl.pallas_call(
        functools.partial(_pool_kernel, seq),
        out_shape=jax.ShapeDtypeStruct((m, d), F32),
        grid=(m // POOL_BT,),
        in_specs=[pl.BlockSpec((POOL_BT, d), row),
                  pl.BlockSpec((HALO, d), lambda i: (jnp.maximum(i * rb - 1, 0), 0)),
                  pl.BlockSpec((HALO, d), lambda i: (jnp.minimum((i + 1) * rb, n_halo_blocks - 1), 0)),
                  pl.BlockSpec((None,) + w_all.shape[1:], lambda i: (layer, 0, 0, 0)),
                  _vec_spec(d, layer), _vec_spec(d, layer),
                  _vec_spec(d, norm_layer), _vec_spec(d, norm_layer)],
        out_specs=pl.BlockSpec((POOL_BT, d), row),
        scratch_shapes=[pltpu.VMEM((POOL_BT, d), F32)],
        compiler_params=pltpu.CompilerParams(
            dimension_semantics=("parallel",), vmem_limit_bytes=VMEM_LIMIT),
        name="pool_mixer_norm",
    )(x, x, x, w_all, bias_all, scale_all, g_all, b_all)


def _mlp_kernel(x_ref, w1_ref, w2_ref, g_ref, b_ref, o_ref, xb_ref):
    j = pl.program_id(1)

    @pl.when(j == 0)
    def _():
        xb_ref[...] = x_ref[...].astype(BF16)
        o_ref[...] = jnp.zeros_like(o_ref)

    hid = jnp.dot(xb_ref[...], w1_ref[...], preferred_element_type=F32)
    hid = jnp.square(jnp.maximum(hid, 0.0)).astype(BF16)
    o_ref[...] += jnp.dot(hid, w2_ref[...].astype(BF16), preferred_element_type=F32)

    @pl.when(j == pl.num_programs(1) - 1)
    def _():
        _residual_norm_inplace(x_ref, g_ref, b_ref, o_ref)


def _mlp_norm(x, w1_all, w2_all, g_all, b_all, layer):
    m, d = x.shape
    f = w2_all.shape[1]
    row = lambda i, j: (i, 0)
    return pl.pallas_call(
        _mlp_kernel,
        out_shape=jax.ShapeDtypeStruct((m, d), F32),
        grid=(m // MLP_BM, f // MLP_BF),
        in_specs=[pl.BlockSpec((MLP_BM, d), row),
                  pl.BlockSpec((None, d, MLP_BF), lambda i, j: (layer, 0, j)),
                  pl.BlockSpec((None, MLP_BF, d), lambda i, j: (layer, j, 0)),
                  _vec_spec(d, layer), _vec_spec(d, layer)],
        out_specs=pl.BlockSpec((MLP_BM, d), row),
        scratch_shapes=[pltpu.VMEM((MLP_BM, d), BF16)],
        compiler_params=pltpu.CompilerParams(
            dimension_semantics=("parallel", "arbitrary"), vmem_limit_bytes=VMEM_LIMIT),
        name="mlp_norm",
    )(x, w1_all, w2_all, g_all, b_all)


def kernel(x, ssd_in_proj, ssd_conv_w, ssd_conv_b, ssd_dt_bias, ssd_A_log, ssd_D, ssd_norm_w,
           ssd_out_proj, pool_w, pool_b, pool_scale, mlp_w1, mlp_w2, ln_mix_g, ln_mix_b,
           ln_ffn_g, ln_ffn_b):
    batch, seq, d = x.shape
    m = batch * seq
    assert d == D_MODEL and seq % (SSD_CPS * CHUNK) == 0 and seq % POOL_BT == 0 and seq % INPROJ_BM == 0
    n_ssd, n_pool = ssd_in_proj.shape[0], pool_w.shape[0]
    vecs = lambda v: v.reshape(v.shape[0], 1, -1)
    w_dt_t = jnp.swapaxes(ssd_in_proj[:, :, D_INNER + D_XBC:], 1, 2)
    w_out = ssd_out_proj.astype(BF16)
    w1 = mlp_w1.astype(BF16)
    conv_w = ssd_conv_w.reshape(n_ssd, CONV_WIDTH, D_XBC)
    conv_b = vecs(ssd_conv_b)
    head_shape = (n_ssd, 2, N_GROUPS, HEADS_PER_GROUP, 1)
    dt_bias = ssd_dt_bias.reshape(head_shape)
    a_log = ssd_A_log.reshape(head_shape)
    d_skip = vecs(jnp.repeat(ssd_D, HEAD_DIM, axis=-1))
    norm_w = vecs(ssd_norm_w)
    pool_b, pool_scale = pool_b.reshape(n_pool, 1, d), vecs(pool_scale)
    ln_mix_g, ln_mix_b, ln_ffn_g, ln_ffn_b = (vecs(v) for v in (ln_mix_g, ln_mix_b, ln_ffn_g, ln_ffn_b))

    xf = x.reshape(m, d)
    for i in range(DEPTH):
        j = i // 2
        if i % 2 == 0:
            xb = xf.astype(BF16)
            z = _in_proj(xb, ssd_in_proj, j, 0, D_INNER, GROUP_DIM)
            xs = _in_proj_conv(xb, ssd_in_proj, conv_w, conv_b, j, seq, 0, D_INNER, GROUP_DIM, F32)
            bc = _in_proj_conv(xb, ssd_in_proj, conv_w, conv_b, j, seq, D_INNER, 2 * D_BC, D_STATE, BF16)
            dt_t = _dt_proj(xb, w_dt_t, j, batch, seq).reshape(batch, 2, N_GROUPS, HEADS_PER_GROUP, seq)
            gy = _ssd_scan(xs, bc, z, dt_t, dt_bias, a_log, d_skip, norm_w, j, batch, seq)
            xf = _out_proj_norm(gy, w_out, xf, ln_mix_g, ln_mix_b, j, i)
        else:
            xf = _pool_mixer_norm(xf, pool_w, pool_b, pool_scale, ln_mix_g, ln_mix_b, j, i, seq)
        xf = _mlp_norm(xf, w1, mlp_w2, ln_ffn_g, ln_ffn_b, i)
    return xf.reshape(batch, seq, d)
```

```python
import functools

import jax
import jax.numpy as jnp
from jax import lax
from jax.experimental import pallas as pl
from jax.experimental.pallas import tpu as pltpu

F32 = jnp.float32
BF16 = jnp.bfloat16

D_MODEL = 2048
DEPTH = 4
D_INNER = 2 * D_MODEL
HEAD_DIM = 64
N_HEADS = D_INNER // HEAD_DIM
N_GROUPS = 8
HEADS_PER_GROUP = N_HEADS // N_GROUPS
GROUP_DIM = D_INNER // N_GROUPS
D_STATE = 128
CONV_WIDTH = 5
CHUNK = 128
D_BC = N_GROUPS * D_STATE
D_XBC = D_INNER + 2 * D_BC
D_IN_PROJ = D_INNER + D_XBC + 2 * N_HEADS
POOL_WINDOWS = (2, 4, 8, 16)
POOL_GROUP_DIM = D_MODEL // len(POOL_WINDOWS)
D_FF = 4 * D_MODEL
DEEPNORM_ALPHA = (2.0 * DEPTH) ** 0.25
LN_EPS = 1e-5
RMS_EPS = 1e-5
LOG2_E = 1.4426950408889634

SUBLANES = 8
LANES = 128
VMEM_LIMIT = 60 * 1024 * 1024

MLP_BM = 1024
MLP_BF = 512
INPROJ_BM = 1024
INPROJ_BN = 1024
OUTPROJ_BM = 1024
OUTPROJ_BK = 1024
CONV_BC = 256
CONV_BR = 128
POOL_BT = 512
SSD_CPS = 16
HALO = SUBLANES
HALO_BF16 = 2 * SUBLANES
SINGLE_BUFFER = pl.Buffered(1)


def _layer_norm(v, g, b):
    mu = jnp.mean(v, axis=-1, keepdims=True)
    d = v - mu
    var = jnp.mean(d * d, axis=-1, keepdims=True)
    return d * lax.rsqrt(var + LN_EPS) * g + b


def _silu(v):
    return v * (1.0 / (1.0 + jnp.exp2(v * -LOG2_E)))


def _split_bf16(v, parts):
    out = []
    r = v
    for _ in range(parts):
        p = r.astype(BF16)
        out.append(p)
        r = r - p.astype(F32)
    return out


def _store_group_major(o_ref, rows, col0, val):
    gw = o_ref.shape[2]
    step = min(gw, val.shape[1])
    for off in range(0, val.shape[1], step):
        c = col0 + off
        o_ref[c // gw, rows, c % gw:c % gw + step] = val[:, off:off + step].astype(o_ref.dtype)


def _store_group_major_t(o_ref, rows, col0, val):
    gw = o_ref.shape[1]
    for off in range(0, val.shape[1], gw):
        c = col0 + off
        o_ref[c // gw, :, rows] = val[:, off:off + gw].T.astype(o_ref.dtype)


def _group_major_out(m, n_cols, group_width, dtype, index_map):
    shape = jax.ShapeDtypeStruct((n_cols // group_width, m, group_width), dtype)
    spec = pl.BlockSpec((INPROJ_BN // group_width, INPROJ_BM, group_width), index_map)
    return shape, spec


def _matmul_kernel(x_ref, w_ref, o_ref):
    res = jnp.dot(x_ref[...], w_ref[...].astype(BF16), preferred_element_type=F32)
    _store_group_major(o_ref, slice(None), 0, res)


def _in_proj(xb, w_all, layer, col_start, n_cols, group_width):
    m, k = xb.shape
    c0 = col_start // INPROJ_BN
    out_shape, out_spec = _group_major_out(m, n_cols, group_width, F32, lambda j, i: (j, i, 0))
    return pl.pallas_call(
        _matmul_kernel,
        out_shape=out_shape,
        grid=(n_cols // INPROJ_BN, m // INPROJ_BM),
        in_specs=[pl.BlockSpec((INPROJ_BM, k), lambda j, i: (i, 0)),
                  pl.BlockSpec((None, k, INPROJ_BN), lambda j, i: (layer, 0, j + c0))],
        out_specs=out_spec,
        compiler_params=pltpu.CompilerParams(
            dimension_semantics=("parallel", "parallel"), vmem_limit_bytes=VMEM_LIMIT),
        name="in_proj",
    )(xb, w_all)


def _dt_proj_kernel(w_ref, x_ref, o_ref):
    o_ref[0] = lax.dot_general(w_ref[...].astype(BF16), x_ref[...], (((1,), (1,)), ((), ())),
                               preferred_element_type=F32)


def _dt_proj(xb, w_dt_t_all, layer, batch, seq):
    m, k = xb.shape
    nh = w_dt_t_all.shape[1]
    seq_tiles = seq // INPROJ_BM
    return pl.pallas_call(
        _dt_proj_kernel,
        out_shape=jax.ShapeDtypeStruct((batch, nh, seq), F32),
        grid=(m // INPROJ_BM,),
        in_specs=[pl.BlockSpec((None, nh, k), lambda i: (layer, 0, 0)),
                  pl.BlockSpec((INPROJ_BM, k), lambda i: (i, 0))],
        out_specs=pl.BlockSpec((1, nh, INPROJ_BM), lambda i: (i // seq_tiles, 0, i % seq_tiles)),
        compiler_params=pltpu.CompilerParams(
            dimension_semantics=("parallel",), vmem_limit_bytes=VMEM_LIMIT),
        name="dt_proj",
    )(w_dt_t_all, xb)


def _in_proj_conv_kernel(seq_tiles, transposed, xm_ref, xp_ref, xn_ref, w_ref, cw_ref, cb_ref, o_ref,
                         xext_ref):
    bt = xm_ref.shape[0]
    hb = HALO_BF16

    @pl.when(pl.program_id(1) == 0)
    def _():
        pos = pl.program_id(0) % seq_tiles
        xext_ref[0:hb, :] = jnp.where(pos == 0, jnp.zeros_like(xp_ref), xp_ref[...])
        xext_ref[hb:hb + bt, :] = xm_ref[...]
        xext_ref[hb + bt:2 * hb + bt, :] = jnp.where(pos == seq_tiles - 1, jnp.zeros_like(xn_ref),
                                                     xn_ref[...])

    pad = CONV_WIDTH // 2
    rb, halo = CONV_BR, HALO
    for c in range(w_ref.shape[1] // CONV_BC):
        cols = slice(c * CONV_BC, (c + 1) * CONV_BC)
        pre = jnp.dot(xext_ref[...], w_ref[:, cols].astype(BF16), preferred_element_type=F32)
        for r in range(0, bt, rb):
            blk = pre[hb + r - halo:hb + r + rb + halo]
            acc = cb_ref[:, cols] + cw_ref[pad:pad + 1, cols] * blk[halo:halo + rb]
            for k in range(CONV_WIDTH):
                if k != pad:
                    shifted = pltpu.roll(blk, (pad - k) % (rb + 2 * halo), axis=0)[halo:halo + rb]
                    acc = acc + cw_ref[k:k + 1, cols] * shifted
            store = _store_group_major_t if transposed else _store_group_major
            store(o_ref, slice(r, r + rb), c * CONV_BC, _silu(acc))


def _in_proj_conv(xb, w_all, conv_w_all, conv_b_all, layer, seq, col_start, n_cols, group_width,
                  out_dtype, transposed=False):
    m, k = xb.shape
    w0 = (D_INNER + col_start) // INPROJ_BN
    c0 = col_start // INPROJ_BN
    rb = INPROJ_BM // HALO_BF16
    n_halo_blocks = m // HALO_BF16
    if transposed:
        assert group_width == CONV_BR
        out_shape = jax.ShapeDtypeStruct((n_cols // group_width, group_width, m), out_dtype)
        out_spec = pl.BlockSpec((INPROJ_BN // group_width, group_width, INPROJ_BM), lambda i, j: (j, 0, i))
    else:
        out_shape, out_spec = _group_major_out(m, n_cols, group_width, out_dtype, lambda i, j: (j, i, 0))
    return pl.pallas_call(
        functools.partial(_in_proj_conv_kernel, seq // INPROJ_BM, transposed),
        out_shape=out_shape,
        grid=(m // INPROJ_BM, n_cols // INPROJ_BN),
        in_specs=[
            pl.BlockSpec((INPROJ_BM, k), lambda i, j: (i, 0)),
            pl.BlockSpec((HALO_BF16, k), lambda i, j: (jnp.maximum(i * rb - 1, 0), 0)),
            pl.BlockSpec((HALO_BF16, k), lambda i, j: (jnp.minimum((i + 1) * rb, n_halo_blocks - 1), 0)),
            pl.BlockSpec((None, k, INPROJ_BN), lambda i, j: (layer, 0, j + w0)),
            pl.BlockSpec((None, CONV_WIDTH, INPROJ_BN), lambda i, j: (layer, 0, j + c0)),
            pl.BlockSpec((None, 1, INPROJ_BN), lambda i, j: (layer, 0, j + c0)),
        ],
        out_specs=out_spec,
        scratch_shapes=[pltpu.VMEM((INPROJ_BM + 2 * HALO_BF16, k), BF16)],
        compiler_params=pltpu.CompilerParams(
            dimension_semantics=("parallel", "arbitrary"), vmem_limit_bytes=VMEM_LIMIT),
        name="in_proj_conv",
    )(xb, xb, xb, w_all, conv_w_all, conv_b_all)


ROW_CS, ROW_WEND, ROW_WIN = (HEADS_PER_GROUP * r for r in range(3))
TABLE_ROWS = 3 * HEADS_PER_GROUP


def _ssd_block(rev, x_ref, bt_ref, c_ref, z_ref, dt_ref, bias_ref, alog_ref, dskip_ref, nw_ref,
               tri_ref, negmask_ref, ef_ref, o_ref, h_ref, yf_ref):
    hpg = HEADS_PER_GROUP
    cl = SSD_CPS * CHUNK
    cb = pl.program_id(3)
    blk = pl.num_programs(3) - 1 - cb if rev else cb
    g0 = pl.multiple_of(blk * cl, cl)
    tri = tri_ref[int(rev)]
    negmask = negmask_ref[int(rev)]

    raw = dt_ref[0, 0, 0] + bias_ref[0, 0]
    dt = jnp.maximum(raw, 0.0) + jnp.log1p(jnp.exp(-jnp.abs(raw)))
    a = dt * (-LOG2_E * jnp.exp(alog_ref[0, 0]))
    log_dt = jnp.log2(dt)
    zero8 = jnp.zeros((hpg, CHUNK), F32)
    parts = []
    for k in range(SSD_CPS):
        parts += [p.astype(F32) for p in _split_bf16(a[:, k * CHUNK:(k + 1) * CHUNK], 3)] + [zero8]
    cs3 = jnp.dot(jnp.concatenate(parts, axis=0).astype(BF16), tri, preferred_element_type=F32)
    pad = jnp.zeros((CHUNK - TABLE_ROWS, CHUNK), F32)
    row_sub, tab_t = [], []
    for k in range(SSD_CPS):
        c3 = cs3[4 * hpg * k:4 * hpg * (k + 1)]
        cs = c3[0:hpg] + c3[hpg:2 * hpg] + c3[2 * hpg:3 * hpg]
        tot = cs[:, 0:1] if rev else cs[:, CHUNK - 1:CHUNK]
        w_end = jnp.exp2(tot - cs) * dt[:, k * CHUNK:(k + 1) * CHUNK]
        w_in = jnp.exp2(cs)
        tab_t.append(jnp.concatenate([cs, w_end, w_in, pad], axis=0).T)
        row_sub.append(cs - log_dt[:, k * CHUNK:(k + 1) * CHUNK])
    tab_t = jnp.concatenate(tab_t, axis=0)
    t_hi, t_lo = _split_bf16(tab_t, 2)
    fe = jnp.dot(jnp.concatenate([t_hi, t_lo], axis=1), ef_ref[...], preferred_element_type=F32)

    lane = lax.broadcasted_iota(jnp.int32, (CHUNK, GROUP_DIM), 1)
    even_head = (lane % (2 * HEAD_DIM)) < HEAD_DIM
    h = h_ref[...]
    for k in (reversed(range(SSD_CPS)) if rev else range(SSD_CPS)):
        rows = slice(k * CHUNK, (k + 1) * CHUNK)
        x = x_ref[rows, :]
        b_t = bt_ref[:, rows]
        cmat = c_ref[rows, :]
        wend_e = fe[rows, 0:GROUP_DIM]
        win_e = fe[rows, GROUP_DIM:2 * GROUP_DIM]
        scores = jnp.dot(cmat, b_t, preferred_element_type=F32)
        x_even = jnp.where(even_head, x, 0.0).astype(BF16)
        x_odd = jnp.where(even_head, 0.0, x).astype(BF16)
        tab_k = tab_t[rows]
        yd = []
        for j in range(hpg // 2):
            m_pair = []
            for hh in (2 * j, 2 * j + 1):
                seg = (jnp.broadcast_to(tab_k[:, ROW_CS + hh:ROW_CS + hh + 1], (CHUNK, CHUNK))
                       - row_sub[k][hh:hh + 1, :])
                m_pair.append((scores * jnp.exp2(seg + negmask)).astype(BF16))
            cols = slice(2 * j * HEAD_DIM, 2 * (j + 1) * HEAD_DIM)
            rhs = jnp.concatenate([x_even[:, cols], x_odd[:, cols]], axis=0)
            yd.append(jnp.dot(jnp.concatenate(m_pair, axis=1), rhs, preferred_element_type=F32))
        y = (jnp.concatenate(yd, axis=1)
             + jnp.dot(cmat, h.astype(BF16), preferred_element_type=F32) * win_e)
        st = jnp.dot(b_t, (x * wend_e).astype(BF16), preferred_element_type=F32)
        chunk_decay = win_e[0:1, :] if rev else win_e[CHUNK - 1:CHUNK, :]
        h = h * chunk_decay + st

        grows = pl.ds(g0 + k * CHUNK, CHUNK)
        if not rev:
            yf_ref[grows, :] = y
        else:
            ytot = yf_ref[grows, :] + y + x * dskip_ref[...]
            gy = ytot * _silu(z_ref[rows, :])
            ms = jnp.mean(gy * gy, axis=-1, keepdims=True)
            o_ref[rows, :] = (gy * lax.rsqrt(ms + RMS_EPS) * nw_ref[...]).astype(o_ref.dtype)
    h_ref[...] = h


def _ssd_kernel(*refs):
    h_ref = refs[-2]

    @pl.when(pl.program_id(3) == 0)
    def _():
        h_ref[...] = jnp.zeros_like(h_ref)

    @pl.when(pl.program_id(2) == 0)
    def _():
        _ssd_block(False, *refs)

    @pl.when(pl.program_id(2) == 1)
    def _():
        _ssd_block(True, *refs)


def _ssd_constants():
    r = lax.broadcasted_iota(jnp.int32, (CHUNK, CHUNK), 0)
    c = lax.broadcasted_iota(jnp.int32, (CHUNK, CHUNK), 1)
    feeds = jnp.stack([c <= r, c >= r])
    negmask = jnp.where(feeds, 0.0, -jnp.inf).astype(F32)
    tri = jnp.swapaxes(feeds, 1, 2).astype(BF16)
    j = lax.broadcasted_iota(jnp.int32, (CHUNK, 2 * GROUP_DIM), 0)
    n = lax.broadcasted_iota(jnp.int32, (CHUNK, 2 * GROUP_DIM), 1)
    ef = (j == ROW_WEND + HEADS_PER_GROUP * (n // GROUP_DIM) + (n % GROUP_DIM) // HEAD_DIM).astype(BF16)
    return tri, negmask, jnp.concatenate([ef, ef], axis=0)


def _ssd_scan(xs, b_t, cm, z, dt_t, dt_bias_all, a_log_all, d_skip_all, norm_w_all, layer, batch, seq):
    m = xs.shape[1]
    cl = SSD_CPS * CHUNK
    n_cb = seq // cl
    tri, negmask, ef = _ssd_constants()

    def in_blk(b, ph, c):
        return b * n_cb + c + ph * (n_cb - 1 - 2 * c)

    def out_blk(b, ph, c):
        return b * n_cb + n_cb - 1 - ph * c

    head_spec = pl.BlockSpec((None, 1, 1, HEADS_PER_GROUP, 1), lambda b, g, ph, c: (layer, ph, g, 0, 0))
    chan_spec = pl.BlockSpec((None, 1, GROUP_DIM), lambda b, g, ph, c: (layer, 0, g))
    return pl.pallas_call(
        _ssd_kernel,
        out_shape=jax.ShapeDtypeStruct((N_GROUPS, m, GROUP_DIM), BF16),
        grid=(batch, N_GROUPS, 2, n_cb),
        in_specs=[
            pl.BlockSpec((None, cl, GROUP_DIM), lambda b, g, ph, c: (g, in_blk(b, ph, c), 0)),
            pl.BlockSpec((None, D_STATE, cl), lambda b, g, ph, c: (g, 0, in_blk(b, ph, c))),
            pl.BlockSpec((None, cl, D_STATE), lambda b, g, ph, c: (g, in_blk(b, ph, c), 0)),
            pl.BlockSpec((None, cl, GROUP_DIM), lambda b, g, ph, c: (g, out_blk(b, ph, c), 0)),
            pl.BlockSpec((1, 1, 1, HEADS_PER_GROUP, cl),
                         lambda b, g, ph, c: (b, ph, g, 0, c + ph * (n_cb - 1 - 2 * c))),
            head_spec, head_spec, chan_spec, chan_spec,
            pl.BlockSpec(tri.shape, lambda b, g, ph, c: (0, 0, 0)),
            pl.BlockSpec(negmask.shape, lambda b, g, ph, c: (0, 0, 0)),
            pl.BlockSpec(ef.shape, lambda b, g, ph, c: (0, 0)),
        ],
        out_specs=pl.BlockSpec((None, cl, GROUP_DIM), lambda b, g, ph, c: (g, out_blk(b, ph, c), 0)),
        scratch_shapes=[pltpu.VMEM((D_STATE, GROUP_DIM), F32),
                        pltpu.VMEM((seq, GROUP_DIM), F32)],
        compiler_params=pltpu.CompilerParams(
            dimension_semantics=("arbitrary", "arbitrary", "arbitrary", "arbitrary"),
            vmem_limit_bytes=VMEM_LIMIT),
        name="ssd_scan",
    )(xs, b_t, cm, z, dt_t, dt_bias_all, a_log_all, d_skip_all, norm_w_all, tri, negmask, ef)


def _vec_spec(n, layer):
    return pl.BlockSpec((None, 1, n), lambda *_: (layer, 0, 0))


def _residual_norm_inplace(x_ref, g_ref, b_ref, o_ref):
    o_ref[...] = _layer_norm(DEEPNORM_ALPHA * x_ref[...] + o_ref[...], g_ref[...], b_ref[...])


def _outproj_kernel(a_ref, w_ref, x_ref, g_ref, b_ref, o_ref):
    j = pl.program_id(1)

    @pl.when(j == 0)
    def _():
        o_ref[...] = jnp.zeros_like(o_ref)

    a = jnp.concatenate([a_ref[g] for g in range(a_ref.shape[0])], axis=1)
    o_ref[...] += jnp.dot(a, w_ref[...], preferred_element_type=F32)

    @pl.when(j == pl.num_programs(1) - 1)
    def _():
        _residual_norm_inplace(x_ref, g_ref, b_ref, o_ref)


def _out_proj_norm(gy, w_all, x, g_all, b_all, layer, norm_layer):
    n_groups, m, gd = gy.shape
    k = n_groups * gd
    n = w_all.shape[2]
    row = lambda i, j: (i, 0)
    return pl.pallas_call(
        _outproj_kernel,
        out_shape=jax.ShapeDtypeStruct((m, n), F32),
        grid=(m // OUTPROJ_BM, k // OUTPROJ_BK),
        in_specs=[pl.BlockSpec((OUTPROJ_BK // gd, OUTPROJ_BM, gd), lambda i, j: (j, i, 0)),
                  pl.BlockSpec((None, OUTPROJ_BK, n), lambda i, j: (layer, j, 0)),
                  pl.BlockSpec((OUTPROJ_BM, n), row),
                  _vec_spec(n, norm_layer), _vec_spec(n, norm_layer)],
        out_specs=pl.BlockSpec((OUTPROJ_BM, n), row),
        compiler_params=pltpu.CompilerParams(
            dimension_semantics=("parallel", "arbitrary"), vmem_limit_bytes=VMEM_LIMIT),
        name="out_proj_norm",
    )(gy, w_all, x, g_all, b_all)


def _pool_kernel(seq, x_ref, p_ref, n_ref, w_ref, bias_ref, scale_ref, g_ref, b_ref,
                 o_ref, v_ref):
    bt = x_ref.shape[0]
    n_ext = bt + 2 * HALO
    seq_tiles = seq // bt
    pos = pl.program_id(0) % seq_tiles
    t = pos * bt + lax.broadcasted_iota(jnp.int32, (bt, 1), 0)
    gd = POOL_GROUP_DIM
    shift_down = lambda v, s: pltpu.roll(v, s % n_ext, axis=0)
    for gi, win in enumerate(POOL_WINDOWS):
        cols = slice(gi * gd, (gi + 1) * gd)
        start = t - win // 2
        cnt = (jnp.clip(start + win, 0, seq) - jnp.clip(start, 0, seq)).astype(F32)
        ext = jnp.concatenate([jnp.where(pos == 0, 0.0, p_ref[:, cols]), x_ref[:, cols],
                               jnp.where(pos == seq_tiles - 1, 0.0, n_ref[:, cols])], axis=0)
        psum, span = ext, 1
        while span < win:
            psum = psum + shift_down(psum, span)
            span *= 2
        ahead = win - win // 2 - 1
        wsum = (shift_down(psum, -ahead) if ahead else psum)[HALO:HALO + bt]
        xg = x_ref[:, cols]
        mg = (wsum / cnt - xg).astype(BF16)
        y = jnp.dot(mg, w_ref[gi].astype(BF16), preferred_element_type=F32) + bias_ref[:, cols]
        v_ref[:, cols] = DEEPNORM_ALPHA * xg + y * scale_ref[:, cols]
    o_ref[...] = _layer_norm(v_ref[...], g_ref[...], b_ref[...])


def _pool_mixer_norm(x, w_all, bias_all, scale_all, g_all, b_all, layer, norm_layer, seq):
    m, d = x.shape
    rb = POOL_BT // HALO
    n_halo_blocks = m // HALO
    row = lambda i: (i, 0)
    return pl.pallas_call(
        functools.partial(_pool_kernel, seq),
        out_shape=jax.ShapeDtypeStruct((m, d), F32),
        grid=(m // POOL_BT,),
        in_specs=[pl.BlockSpec((POOL_BT, d), row),
                  pl.BlockSpec((HALO, d), lambda i: (jnp.maximum(i * rb - 1, 0), 0)),
                  pl.BlockSpec((HALO, d), lambda i: (jnp.minimum((i + 1) * rb, n_halo_blocks - 1), 0)),
                  pl.BlockSpec((None,) + w_all.shape[1:], lambda i: (layer, 0, 0, 0)),
                  _vec_spec(d, layer), _vec_spec(d, layer),
                  _vec_spec(d, norm_layer), _vec_spec(d, norm_layer)],
        out_specs=pl.BlockSpec((POOL_BT, d), row),
        scratch_shapes=[pltpu.VMEM((POOL_BT, d), F32)],
        compiler_params=pltpu.CompilerParams(
            dimension_semantics=("parallel",), vmem_limit_bytes=VMEM_LIMIT),
        name="pool_mixer_norm",
    )(x, x, x, w_all, bias_all, scale_all, g_all, b_all)


def _mlp_kernel(x_ref, w1_ref, w2_ref, g_ref, b_ref, o_ref, xb_ref):
    j = pl.program_id(1)

    @pl.when(j == 0)
    def _():
        xb_ref[...] = x_ref[...].astype(BF16)
        o_ref[...] = jnp.zeros_like(o_ref)

    hid = jnp.dot(xb_ref[...], w1_ref[...], preferred_element_type=F32)
    hid = jnp.square(jnp.maximum(hid, 0.0)).astype(BF16)
    o_ref[...] += jnp.dot(hid, w2_ref[...].astype(BF16), preferred_element_type=F32)

    @pl.when(j == pl.num_programs(1) - 1)
    def _():
        _residual_norm_inplace(x_ref, g_ref, b_ref, o_ref)


def _mlp_norm(x, w1_all, w2_all, g_all, b_all, layer):
    m, d = x.shape
    f = w2_all.shape[1]
    row = lambda i, j: (i, 0)
    return pl.pallas_call(
        _mlp_kernel,
        out_shape=jax.ShapeDtypeStruct((m, d), F32),
        grid=(m // MLP_BM, f // MLP_BF),
        in_specs=[pl.BlockSpec((MLP_BM, d), row),
                  pl.BlockSpec((None, d, MLP_BF), lambda i, j: (layer, 0, j)),
                  pl.BlockSpec((None, MLP_BF, d), lambda i, j: (layer, j, 0)),
                  _vec_spec(d, layer), _vec_spec(d, layer)],
        out_specs=pl.BlockSpec((MLP_BM, d), row),
        scratch_shapes=[pltpu.VMEM((MLP_BM, d), BF16)],
        compiler_params=pltpu.CompilerParams(
            dimension_semantics=("parallel", "arbitrary"), vmem_limit_bytes=VMEM_LIMIT),
        name="mlp_norm",
    )(x, w1_all, w2_all, g_all, b_all)


def kernel(x, ssd_in_proj, ssd_conv_w, ssd_conv_b, ssd_dt_bias, ssd_A_log, ssd_D, ssd_norm_w,
           ssd_out_proj, pool_w, pool_b, pool_scale, mlp_w1, mlp_w2, ln_mix_g, ln_mix_b,
           ln_ffn_g, ln_ffn_b):
    batch, seq, d = x.shape
    m = batch * seq
    assert d == D_MODEL and seq % (SSD_CPS * CHUNK) == 0 and seq % POOL_BT == 0 and seq % INPROJ_BM == 0
    n_ssd, n_pool = ssd_in_proj.shape[0], pool_w.shape[0]
    vecs = lambda v: v.reshape(v.shape[0], 1, -1)
    w_dt_t = jnp.swapaxes(ssd_in_proj[:, :, D_INNER + D_XBC:], 1, 2)
    w_out = ssd_out_proj.astype(BF16)
    w1 = mlp_w1.astype(BF16)
    conv_w = ssd_conv_w.reshape(n_ssd, CONV_WIDTH, D_XBC)
    conv_b = vecs(ssd_conv_b)
    head_shape = (n_ssd, 2, N_GROUPS, HEADS_PER_GROUP, 1)
    dt_bias = ssd_dt_bias.reshape(head_shape)
    a_log = ssd_A_log.reshape(head_shape)
    d_skip = vecs(jnp.repeat(ssd_D, HEAD_DIM, axis=-1))
    norm_w = vecs(ssd_norm_w)
    pool_b, pool_scale = pool_b.reshape(n_pool, 1, d), vecs(pool_scale)
    ln_mix_g, ln_mix_b, ln_ffn_g, ln_ffn_b = (vecs(v) for v in (ln_mix_g, ln_mix_b, ln_ffn_g, ln_ffn_b))

    xf = x.reshape(m, d)
    for i in range(DEPTH):
        j = i // 2
        if i % 2 == 0:
            xb = xf.astype(BF16)
            z = _in_proj(xb, ssd_in_proj, j, 0, D_INNER, GROUP_DIM)
            xs = _in_proj_conv(xb, ssd_in_proj, conv_w, conv_b, j, seq, 0, D_INNER, GROUP_DIM, F32)
            b_t = _in_proj_conv(xb, ssd_in_proj, conv_w, conv_b, j, seq, D_INNER, D_BC, D_STATE, BF16,
                                transposed=True)
            cm = _in_proj_conv(xb, ssd_in_proj, conv_w, conv_b, j, seq, D_INNER + D_BC, D_BC, D_STATE, BF16)
            dt_t = _dt_proj(xb, w_dt_t, j, batch, seq).reshape(batch, 2, N_GROUPS, HEADS_PER_GROUP, seq)
            gy = _ssd_scan(xs, b_t, cm, z, dt_t, dt_bias, a_log, d_skip, norm_w, j, batch, seq)
            xf = _out_proj_norm(gy, w_out, xf, ln_mix_g, ln_mix_b, j, i)
        else:
            xf = _pool_mixer_norm(xf, pool_w, pool_b, pool_scale, ln_mix_g, ln_mix_b, j, i, seq)
        xf = _mlp_norm(xf, w1, mlp_w2, ln_ffn_g, ln_ffn_b, i)
    return xf.reshape(batch, seq, d)
```

```python
import functools

import jax
import jax.numpy as jnp
from jax import lax
from jax.experimental import pallas as pl
from jax.experimental.pallas import tpu as pltpu

F32 = jnp.float32
BF16 = jnp.bfloat16

D_MODEL = 2048
DEPTH = 4
D_INNER = 2 * D_MODEL
HEAD_DIM = 64
N_HEADS = D_INNER // HEAD_DIM
N_GROUPS = 8
HEADS_PER_GROUP = N_HEADS // N_GROUPS
GROUP_DIM = D_INNER // N_GROUPS
D_STATE = 128
CONV_WIDTH = 5
CHUNK = 128
D_BC = N_GROUPS * D_STATE
D_XBC = D_INNER + 2 * D_BC
D_IN_PROJ = D_INNER + D_XBC + 2 * N_HEADS
POOL_WINDOWS = (2, 4, 8, 16)
POOL_GROUP_DIM = D_MODEL // len(POOL_WINDOWS)
D_FF = 4 * D_MODEL
DEEPNORM_ALPHA = (2.0 * DEPTH) ** 0.25
LN_EPS = 1e-5
RMS_EPS = 1e-5
LOG2_E = 1.4426950408889634

SUBLANES = 8
LANES = 128
VMEM_LIMIT = 60 * 1024 * 1024

MLP_BM = 1024
MLP_BF = 512
INPROJ_BM = 1024
INPROJ_BN = 1024
OUTPROJ_BM = 1024
OUTPROJ_BK = 1024
CONV_BC = 256
CONV_BR = 128
POOL_BT = 512
SSD_CPS = 16
HALO = SUBLANES
HALO_BF16 = 2 * SUBLANES
SINGLE_BUFFER = pl.Buffered(1)


def _layer_norm(v, g, b):
    mu = jnp.mean(v, axis=-1, keepdims=True)
    d = v - mu
    var = jnp.mean(d * d, axis=-1, keepdims=True)
    return d * lax.rsqrt(var + LN_EPS) * g + b


def _silu(v):
    return v * (1.0 / (1.0 + jnp.exp2(v * -LOG2_E)))


def _split_bf16(v, parts):
    out = []
    r = v
    for _ in range(parts):
        p = r.astype(BF16)
        out.append(p)
        r = r - p.astype(F32)
    return out


def _store_group_major(o_ref, rows, col0, val):
    gw = o_ref.shape[2]
    step = min(gw, val.shape[1])
    for off in range(0, val.shape[1], step):
        c = col0 + off
        o_ref[c // gw, rows, c % gw:c % gw + step] = val[:, off:off + step].astype(o_ref.dtype)


def _store_group_major_t(o_ref, rows, col0, val):
    gw = o_ref.shape[1]
    for off in range(0, val.shape[1], gw):
        c = col0 + off
        o_ref[c // gw, :, rows] = val[:, off:off + gw].T.astype(o_ref.dtype)


def _group_major_out(m, n_cols, group_width, dtype, index_map):
    shape = jax.ShapeDtypeStruct((n_cols // group_width, m, group_width), dtype)
    spec = pl.BlockSpec((INPROJ_BN // group_width, INPROJ_BM, group_width), index_map)
    return shape, spec


def _matmul_kernel(x_ref, w_ref, o_ref):
    res = jnp.dot(x_ref[...], w_ref[...].astype(BF16), preferred_element_type=F32)
    _store_group_major(o_ref, slice(None), 0, res)


def _in_proj(xb, w_all, layer, col_start, n_cols, group_width):
    m, k = xb.shape
    c0 = col_start // INPROJ_BN
    out_shape, out_spec = _group_major_out(m, n_cols, group_width, F32, lambda j, i: (j, i, 0))
    return pl.pallas_call(
        _matmul_kernel,
        out_shape=out_shape,
        grid=(n_cols // INPROJ_BN, m // INPROJ_BM),
        in_specs=[pl.BlockSpec((INPROJ_BM, k), lambda j, i: (i, 0)),
                  pl.BlockSpec((None, k, INPROJ_BN), lambda j, i: (layer, 0, j + c0))],
        out_specs=out_spec,
        compiler_params=pltpu.CompilerParams(
            dimension_semantics=("parallel", "parallel"), vmem_limit_bytes=VMEM_LIMIT),
        name="in_proj",
    )(xb, w_all)


def _dt_proj_kernel(w_ref, x_ref, o_ref):
    o_ref[0] = lax.dot_general(w_ref[...].astype(BF16), x_ref[...], (((1,), (1,)), ((), ())),
                               preferred_element_type=F32)


def _dt_proj(xb, w_dt_t_all, layer, batch, seq):
    m, k = xb.shape
    nh = w_dt_t_all.shape[1]
    seq_tiles = seq // INPROJ_BM
    return pl.pallas_call(
        _dt_proj_kernel,
        out_shape=jax.ShapeDtypeStruct((batch, nh, seq), F32),
        grid=(m // INPROJ_BM,),
        in_specs=[pl.BlockSpec((None, nh, k), lambda i: (layer, 0, 0)),
                  pl.BlockSpec((INPROJ_BM, k), lambda i: (i, 0))],
        out_specs=pl.BlockSpec((1, nh, INPROJ_BM), lambda i: (i // seq_tiles, 0, i % seq_tiles)),
        compiler_params=pltpu.CompilerParams(
            dimension_semantics=("parallel",), vmem_limit_bytes=VMEM_LIMIT),
        name="dt_proj",
    )(w_dt_t_all, xb)


def _in_proj_conv_kernel(seq_tiles, transposed, xm_ref, xp_ref, xn_ref, w_ref, cw_ref, cb_ref, o_ref,
                         xext_ref):
    bt = xm_ref.shape[0]
    hb = HALO_BF16

    @pl.when(pl.program_id(1) == 0)
    def _():
        pos = pl.program_id(0) % seq_tiles
        xext_ref[0:hb, :] = jnp.where(pos == 0, jnp.zeros_like(xp_ref), xp_ref[...])
        xext_ref[hb:hb + bt, :] = xm_ref[...]
        xext_ref[hb + bt:2 * hb + bt, :] = jnp.where(pos == seq_tiles - 1, jnp.zeros_like(xn_ref),
                                                     xn_ref[...])

    pad = CONV_WIDTH // 2
    rb, halo = CONV_BR, HALO
    for c in range(w_ref.shape[1] // CONV_BC):
        cols = slice(c * CONV_BC, (c + 1) * CONV_BC)
        pre = jnp.dot(xext_ref[...], w_ref[:, cols].astype(BF16), preferred_element_type=F32)
        for r in range(0, bt, rb):
            blk = pre[hb + r - halo:hb + r + rb + halo]
            acc = cb_ref[:, cols] + cw_ref[pad:pad + 1, cols] * blk[halo:halo + rb]
            for k in range(CONV_WIDTH):
                if k != pad:
                    shifted = pltpu.roll(blk, (pad - k) % (rb + 2 * halo), axis=0)[halo:halo + rb]
                    acc = acc + cw_ref[k:k + 1, cols] * shifted
            store = _store_group_major_t if transposed else _store_group_major
            store(o_ref, slice(r, r + rb), c * CONV_BC, _silu(acc))


def _in_proj_conv(xb, w_all, conv_w_all, conv_b_all, layer, seq, col_start, n_cols, group_width,
                  out_dtype, transposed=False):
    m, k = xb.shape
    w0 = (D_INNER + col_start) // INPROJ_BN
    c0 = col_start // INPROJ_BN
    rb = INPROJ_BM // HALO_BF16
    n_halo_blocks = m // HALO_BF16
    if transposed:
        assert group_width == CONV_BR
        out_shape = jax.ShapeDtypeStruct((n_cols // group_width, group_width, m), out_dtype)
        out_spec = pl.BlockSpec((INPROJ_BN // group_width, group_width, INPROJ_BM), lambda i, j: (j, 0, i))
    else:
        out_shape, out_spec = _group_major_out(m, n_cols, group_width, out_dtype, lambda i, j: (j, i, 0))
    return pl.pallas_call(
        functools.partial(_in_proj_conv_kernel, seq // INPROJ_BM, transposed),
        out_shape=out_shape,
        grid=(m // INPROJ_BM, n_cols // INPROJ_BN),
        in_specs=[
            pl.BlockSpec((INPROJ_BM, k), lambda i, j: (i, 0)),
            pl.BlockSpec((HALO_BF16, k), lambda i, j: (jnp.maximum(i * rb - 1, 0), 0)),
            pl.BlockSpec((HALO_BF16, k), lambda i, j: (jnp.minimum((i + 1) * rb, n_halo_blocks - 1), 0)),
            pl.BlockSpec((None, k, INPROJ_BN), lambda i, j: (layer, 0, j + w0)),
            pl.BlockSpec((None, CONV_WIDTH, INPROJ_BN), lambda i, j: (layer, 0, j + c0)),
            pl.BlockSpec((None, 1, INPROJ_BN), lambda i, j: (layer, 0, j + c0)),
        ],
        out_specs=out_spec,
        scratch_shapes=[pltpu.VMEM((INPROJ_BM + 2 * HALO_BF16, k), BF16)],
        compiler_params=pltpu.CompilerParams(
            dimension_semantics=("parallel", "arbitrary"), vmem_limit_bytes=VMEM_LIMIT),
        name="in_proj_conv",
    )(xb, xb, xb, w_all, conv_w_all, conv_b_all)


ROW_CS, ROW_WEND, ROW_WIN = (HEADS_PER_GROUP * r for r in range(3))
TABLE_ROWS = 3 * HEADS_PER_GROUP


def _ssd_block(rev, x_ref, bt_ref, c_ref, z_ref, dt_ref, bias_ref, alog_ref, dskip_ref, nw_ref,
               tri_ref, negmask_ref, ef_ref, o_ref, h_ref, yf_ref):
    hpg = HEADS_PER_GROUP
    cl = SSD_CPS * CHUNK
    cb = pl.program_id(3)
    blk = pl.num_programs(3) - 1 - cb if rev else cb
    g0 = pl.multiple_of(blk * cl, cl)
    tri = tri_ref[int(rev)]
    negmask = negmask_ref[int(rev)]

    raw = dt_ref[0, 0, 0] + bias_ref[0, 0]
    dt = jnp.maximum(raw, 0.0) + jnp.log1p(jnp.exp(-jnp.abs(raw)))
    a = dt * (-LOG2_E * jnp.exp(alog_ref[0, 0]))
    log_dt = jnp.log2(dt)
    zero8 = jnp.zeros((hpg, CHUNK), F32)
    parts = []
    for k in range(SSD_CPS):
        parts += [p.astype(F32) for p in _split_bf16(a[:, k * CHUNK:(k + 1) * CHUNK], 3)] + [zero8]
    cs3 = jnp.dot(jnp.concatenate(parts, axis=0).astype(BF16), tri, preferred_element_type=F32)
    pad = jnp.zeros((CHUNK - TABLE_ROWS, CHUNK), F32)
    row_sub, tab_t = [], []
    for k in range(SSD_CPS):
        c3 = cs3[4 * hpg * k:4 * hpg * (k + 1)]
        cs = c3[0:hpg] + c3[hpg:2 * hpg] + c3[2 * hpg:3 * hpg]
        tot = cs[:, 0:1] if rev else cs[:, CHUNK - 1:CHUNK]
        w_end = jnp.exp2(tot - cs) * dt[:, k * CHUNK:(k + 1) * CHUNK]
        w_in = jnp.exp2(cs)
        tab_t.append(jnp.concatenate([cs, w_end, w_in, pad], axis=0).T)
        row_sub.append(cs - log_dt[:, k * CHUNK:(k + 1) * CHUNK])
    tab_t = jnp.concatenate(tab_t, axis=0)
    t_hi, t_lo = _split_bf16(tab_t, 2)
    fe = jnp.dot(jnp.concatenate([t_hi, t_lo], axis=1), ef_ref[...], preferred_element_type=F32)

    lane = lax.broadcasted_iota(jnp.int32, (CHUNK, GROUP_DIM), 1)
    even_head = (lane % (2 * HEAD_DIM)) < HEAD_DIM
    h = h_ref[...]
    for k in (reversed(range(SSD_CPS)) if rev else range(SSD_CPS)):
        rows = slice(k * CHUNK, (k + 1) * CHUNK)
        x = x_ref[rows, :]
        b_t = bt_ref[:, rows]
        cmat = c_ref[rows, :]
        wend_e = fe[rows, 0:GROUP_DIM]
        win_e = fe[rows, GROUP_DIM:2 * GROUP_DIM]
        scores = jnp.dot(cmat, b_t, preferred_element_type=F32)
        x_even = jnp.where(even_head, x, 0.0).astype(BF16)
        x_odd = jnp.where(even_head, 0.0, x).astype(BF16)
        tab_k = tab_t[rows]
        yd = []
        for j in range(hpg // 2):
            m_pair = []
            for hh in (2 * j, 2 * j + 1):
                seg = (jnp.broadcast_to(tab_k[:, ROW_CS + hh:ROW_CS + hh + 1], (CHUNK, CHUNK))
                       - row_sub[k][hh:hh + 1, :])
                m_pair.append((scores * jnp.exp2(seg + negmask)).astype(BF16))
            cols = slice(2 * j * HEAD_DIM, 2 * (j + 1) * HEAD_DIM)
            rhs = jnp.concatenate([x_even[:, cols], x_odd[:, cols]], axis=0)
            yd.append(jnp.dot(jnp.concatenate(m_pair, axis=1), rhs, preferred_element_type=F32))
        y = (jnp.concatenate(yd, axis=1)
             + jnp.dot(cmat, h.astype(BF16), preferred_element_type=F32) * win_e)
        st = jnp.dot(b_t, (x * wend_e).astype(BF16), preferred_element_type=F32)
        chunk_decay = win_e[0:1, :] if rev else win_e[CHUNK - 1:CHUNK, :]
        h = h * chunk_decay + st

        grows = pl.ds(g0 + k * CHUNK, CHUNK)
        if not rev:
            yf_ref[grows, :] = y
        else:
            ytot = yf_ref[grows, :] + y + x * dskip_ref[...]
            gy = ytot * _silu(z_ref[rows, :])
            ms = jnp.mean(gy * gy, axis=-1, keepdims=True)
            o_ref[rows, :] = (gy * lax.rsqrt(ms + RMS_EPS) * nw_ref[...]).astype(o_ref.dtype)
    h_ref[...] = h


def _ssd_kernel(*refs):
    h_ref = refs[-2]

    @pl.when(pl.program_id(3) == 0)
    def _():
        h_ref[...] = jnp.zeros_like(h_ref)

    @pl.when(pl.program_id(2) == 0)
    def _():
        _ssd_block(False, *refs)

    @pl.when(pl.program_id(2) == 1)
    def _():
        _ssd_block(True, *refs)


def _ssd_constants():
    r = lax.broadcasted_iota(jnp.int32, (CHUNK, CHUNK), 0)
    c = lax.broadcasted_iota(jnp.int32, (CHUNK, CHUNK), 1)
    feeds = jnp.stack([c <= r, c >= r])
    negmask = jnp.where(feeds, 0.0, -jnp.inf).astype(F32)
    tri = jnp.swapaxes(feeds, 1, 2).astype(BF16)
    j = lax.broadcasted_iota(jnp.int32, (CHUNK, 2 * GROUP_DIM), 0)
    n = lax.broadcasted_iota(jnp.int32, (CHUNK, 2 * GROUP_DIM), 1)
    ef = (j == ROW_WEND + HEADS_PER_GROUP * (n // GROUP_DIM) + (n % GROUP_DIM) // HEAD_DIM).astype(BF16)
    return tri, negmask, jnp.concatenate([ef, ef], axis=0)


def _ssd_scan(xs, b_t, cm, z, dt_t, dt_bias_all, a_log_all, d_skip_all, norm_w_all, layer, batch, seq):
    m = xs.shape[1]
    cl = SSD_CPS * CHUNK
    n_cb = seq // cl
    tri, negmask, ef = _ssd_constants()

    def in_blk(b, ph, c):
        return b * n_cb + c + ph * (n_cb - 1 - 2 * c)

    def out_blk(b, ph, c):
        return b * n_cb + n_cb - 1 - ph * c

    head_spec = pl.BlockSpec((None, 1, 1, HEADS_PER_GROUP, 1), lambda b, g, ph, c: (layer, ph, g, 0, 0))
    chan_spec = pl.BlockSpec((None, 1, GROUP_DIM), lambda b, g, ph, c: (layer, 0, g))
    return pl.pallas_call(
        _ssd_kernel,
        out_shape=jax.ShapeDtypeStruct((N_GROUPS, m, GROUP_DIM), BF16),
        grid=(batch, N_GROUPS, 2, n_cb),
        in_specs=[
            pl.BlockSpec((None, cl, GROUP_DIM), lambda b, g, ph, c: (g, in_blk(b, ph, c), 0)),
            pl.BlockSpec((None, D_STATE, cl), lambda b, g, ph, c: (g, 0, in_blk(b, ph, c))),
            pl.BlockSpec((None, cl, D_STATE), lambda b, g, ph, c: (g, in_blk(b, ph, c), 0)),
            pl.BlockSpec((None, cl, GROUP_DIM), lambda b, g, ph, c: (g, out_blk(b, ph, c), 0)),
            pl.BlockSpec((1, 1, 1, HEADS_PER_GROUP, cl),
                         lambda b, g, ph, c: (b, ph, g, 0, c + ph * (n_cb - 1 - 2 * c))),
            head_spec, head_spec, chan_spec, chan_spec,
            pl.BlockSpec(tri.shape, lambda b, g, ph, c: (0, 0, 0)),
            pl.BlockSpec(negmask.shape, lambda b, g, ph, c: (0, 0, 0)),
            pl.BlockSpec(ef.shape, lambda b, g, ph, c: (0, 0)),
        ],
        out_specs=pl.BlockSpec((None, cl, GROUP_DIM), lambda b, g, ph, c: (g, out_blk(b, ph, c), 0)),
        scratch_shapes=[pltpu.VMEM((D_STATE, GROUP_DIM), F32),
                        pltpu.VMEM((seq, GROUP_DIM), F32)],
        compiler_params=pltpu.CompilerParams(
            dimension_semantics=("arbitrary", "arbitrary", "arbitrary", "arbitrary"),
            vmem_limit_bytes=VMEM_LIMIT),
        name="ssd_scan",
    )(xs, b_t, cm, z, dt_t, dt_bias_all, a_log_all, d_skip_all, norm_w_all, tri, negmask, ef)


def _vec_spec(n, layer):
    return pl.BlockSpec((None, 1, n), lambda *_: (layer, 0, 0))


def _residual_norm_inplace(x_ref, g_ref, b_ref, o_ref):
    o_ref[...] = _layer_norm(DEEPNORM_ALPHA * x_ref[...] + o_ref[...], g_ref[...], b_ref[...])


def _outproj_kernel(a_ref, w_ref, x_ref, g_ref, b_ref, o_ref):
    j = pl.program_id(1)

    @pl.when(j == 0)
    def _():
        o_ref[...] = jnp.zeros_like(o_ref)

    a = jnp.concatenate([a_ref[g] for g in range(a_ref.shape[0])], axis=1)
    o_ref[...] += jnp.dot(a, w_ref[...], preferred_element_type=F32)

    @pl.when(j == pl.num_programs(1) - 1)
    def _():
        _residual_norm_inplace(x_ref, g_ref, b_ref, o_ref)


def _out_proj_norm(gy, w_all, x, g_all, b_all, layer, norm_layer):
    n_groups, m, gd = gy.shape
    k = n_groups * gd
    n = w_all.shape[2]
    row = lambda i, j: (i, 0)
    return pl.pallas_call(
        _outproj_kernel,
        out_shape=jax.ShapeDtypeStruct((m, n), F32),
        grid=(m // OUTPROJ_BM, k // OUTPROJ_BK),
        in_specs=[pl.BlockSpec((OUTPROJ_BK // gd, OUTPROJ_BM, gd), lambda i, j: (j, i, 0)),
                  pl.BlockSpec((None, OUTPROJ_BK, n), lambda i, j: (layer, j, 0)),
                  pl.BlockSpec((OUTPROJ_BM, n), row),
                  _vec_spec(n, norm_layer), _vec_spec(n, norm_layer)],
        out_specs=pl.BlockSpec((OUTPROJ_BM, n), row),
        compiler_params=pltpu.CompilerParams(
            dimension_semantics=("parallel", "arbitrary"), vmem_limit_bytes=VMEM_LIMIT),
        name="out_proj_norm",
    )(gy, w_all, x, g_all, b_all)


def _pool_kernel(seq, x_ref, p_ref, n_ref, w_ref, bias_ref, scale_ref, g_ref, b_ref,
                 o_ref, v_ref):
    bt = x_ref.shape[0]
    n_ext = bt + 2 * HALO
    seq_tiles = seq // bt
    pos = pl.program_id(0) % seq_tiles
    t = pos * bt + lax.broadcasted_iota(jnp.int32, (bt, 1), 0)
    gd = POOL_GROUP_DIM
    shift_down = lambda v, s: pltpu.roll(v, s % n_ext, axis=0)
    for gi, win in enumerate(POOL_WINDOWS):
        cols = slice(gi * gd, (gi + 1) * gd)
        start = t - win // 2
        cnt = (jnp.clip(start + win, 0, seq) - jnp.clip(start, 0, seq)).astype(F32)
        ext = jnp.concatenate([jnp.where(pos == 0, 0.0, p_ref[:, cols]), x_ref[:, cols],
                               jnp.where(pos == seq_tiles - 1, 0.0, n_ref[:, cols])], axis=0)
        psum, span = ext, 1
        while span < win:
            psum = psum + shift_down(psum, span)
            span *= 2
        ahead = win - win // 2 - 1
        wsum = (shift_down(psum, -ahead) if ahead else psum)[HALO:HALO + bt]
        xg = x_ref[:, cols]
        mg = (wsum / cnt - xg).astype(BF16)
        y = jnp.dot(mg, w_ref[gi].astype(BF16), preferred_element_type=F32) + bias_ref[:, cols]
        v_ref[:, cols] = DEEPNORM_ALPHA * xg + y * scale_ref[:, cols]
    o_ref[...] = _layer_norm(v_ref[...], g_ref[...], b_ref[...])


def _pool_mixer_norm(x, w_all, bias_all, scale_all, g_all, b_all, layer, norm_layer, seq):
    m, d = x.shape
    rb = POOL_BT // HALO
    n_halo_blocks = m // HALO
    row = lambda i: (i, 0)
    return pl.pallas_call(
        functools.partial(_pool_kernel, seq),
        out_shape=jax.ShapeDtypeStruct((m, d), F32),
        grid=(m // POOL_BT,),
        in_specs=[pl.BlockSpec((POOL_BT, d), row),
                  pl.BlockSpec((HALO, d), lambda i: (jnp.maximum(i * rb - 1, 0), 0)),
                  pl.BlockSpec((HALO, d), lambda i: (jnp.minimum((i + 1) * rb, n_halo_blocks - 1), 0)),
                  pl.BlockSpec((None,) + w_all.shape[1:], lambda i: (layer, 0, 0, 0)),
                  _vec_spec(d, layer), _vec_spec(d, layer),
                  _vec_spec(d, norm_layer), _vec_spec(d, norm_layer)],
        out_specs=pl.BlockSpec((POOL_BT, d), row),
        scratch_shapes=[pltpu.VMEM((POOL_BT, d), F32)],
        compiler_params=pltpu.CompilerParams(
            dimension_semantics=("parallel",), vmem_limit_bytes=VMEM_LIMIT),
        name="pool_mixer_norm",
    )(x, x, x, w_all, bias_all, scale_all, g_all, b_all)


def _mlp_kernel(x_ref, w1_ref, w2_ref, g_ref, b_ref, o_ref, xb_ref):
    j = pl.program_id(1)

    @pl.when(j == 0)
    def _():
        xb_ref[...] = x_ref[...].astype(BF16)
        o_ref[...] = jnp.zeros_like(o_ref)

    hid = jnp.dot(xb_ref[...], w1_ref[...], preferred_element_type=F32)
    hid = jnp.square(jnp.maximum(hid, 0.0)).astype(BF16)
    o_ref[...] += jnp.dot(hid, w2_ref[...], preferred_element_type=F32)

    @pl.when(j == pl.num_programs(1) - 1)
    def _():
        _residual_norm_inplace(x_ref, g_ref, b_ref, o_ref)


def _mlp_norm(x, w1_all, w2_all, g_all, b_all, layer):
    m, d = x.shape
    f = w2_all.shape[1]
    row = lambda i, j: (i, 0)
    return pl.pallas_call(
        _mlp_kernel,
        out_shape=jax.ShapeDtypeStruct((m, d), F32),
        grid=(m // MLP_BM, f // MLP_BF),
        in_specs=[pl.BlockSpec((MLP_BM, d), row),
                  pl.BlockSpec((None, d, MLP_BF), lambda i, j: (layer, 0, j)),
                  pl.BlockSpec((None, MLP_BF, d), lambda i, j: (layer, j, 0)),
                  _vec_spec(d, layer), _vec_spec(d, layer)],
        out_specs=pl.BlockSpec((MLP_BM, d), row),
        scratch_shapes=[pltpu.VMEM((MLP_BM, d), BF16)],
        compiler_params=pltpu.CompilerParams(
            dimension_semantics=("parallel", "arbitrary"), vmem_limit_bytes=VMEM_LIMIT),
        name="mlp_norm",
    )(x, w1_all, w2_all, g_all, b_all)


def kernel(x, ssd_in_proj, ssd_conv_w, ssd_conv_b, ssd_dt_bias, ssd_A_log, ssd_D, ssd_norm_w,
           ssd_out_proj, pool_w, pool_b, pool_scale, mlp_w1, mlp_w2, ln_mix_g, ln_mix_b,
           ln_ffn_g, ln_ffn_b):
    batch, seq, d = x.shape
    m = batch * seq
    assert d == D_MODEL and seq % (SSD_CPS * CHUNK) == 0 and seq % POOL_BT == 0 and seq % INPROJ_BM == 0
    n_ssd, n_pool = ssd_in_proj.shape[0], pool_w.shape[0]
    vecs = lambda v: v.reshape(v.shape[0], 1, -1)
    w_dt_t = jnp.swapaxes(ssd_in_proj[:, :, D_INNER + D_XBC:], 1, 2)
    w_out = ssd_out_proj.astype(BF16)
    w1, w2 = mlp_w1.astype(BF16), mlp_w2.astype(BF16)
    conv_w = ssd_conv_w.reshape(n_ssd, CONV_WIDTH, D_XBC)
    conv_b = vecs(ssd_conv_b)
    head_shape = (n_ssd, 2, N_GROUPS, HEADS_PER_GROUP, 1)
    dt_bias = ssd_dt_bias.reshape(head_shape)
    a_log = ssd_A_log.reshape(head_shape)
    d_skip = vecs(jnp.repeat(ssd_D, HEAD_DIM, axis=-1))
    norm_w = vecs(ssd_norm_w)
    pool_b, pool_scale = pool_b.reshape(n_pool, 1, d), vecs(pool_scale)
    ln_mix_g, ln_mix_b, ln_ffn_g, ln_ffn_b = (vecs(v) for v in (ln_mix_g, ln_mix_b, ln_ffn_g, ln_ffn_b))

    xf = x.reshape(m, d)
    for i in range(DEPTH):
        j = i // 2
        if i % 2 == 0:
            xb = xf.astype(BF16)
            z = _in_proj(xb, ssd_in_proj, j, 0, D_INNER, GROUP_DIM)
            xs = _in_proj_conv(xb, ssd_in_proj, conv_w, conv_b, j, seq, 0, D_INNER, GROUP_DIM, F32)
            b_t = _in_proj_conv(xb, ssd_in_proj, conv_w, conv_b, j, seq, D_INNER, D_BC, D_STATE, BF16,
                                transposed=True)
            cm = _in_proj_conv(xb, ssd_in_proj, conv_w, conv_b, j, seq, D_INNER + D_BC, D_BC, D_STATE, BF16)
            dt_t = _dt_proj(xb, w_dt_t, j, batch, seq).reshape(batch, 2, N_GROUPS, HEADS_PER_GROUP, seq)
            gy = _ssd_scan(xs, b_t, cm, z, dt_t, dt_bias, a_log, d_skip, norm_w, j, batch, seq)
            xf = _out_proj_norm(gy, w_out, xf, ln_mix_g, ln_mix_b, j, i)
        else:
            xf = _pool_mixer_norm(xf, pool_w, pool_b, pool_scale, ln_mix_g, ln_mix_b, j, i, seq)
        xf = _mlp_norm(xf, w1, w2, ln_ffn_g, ln_ffn_b, i)
    return xf.reshape(batch, seq, d)
```

```python
import functools

import jax
import jax.numpy as jnp
from jax import lax
from jax.experimental import pallas as pl
from jax.experimental.pallas import tpu as pltpu

F32 = jnp.float32
BF16 = jnp.bfloat16

D_MODEL = 2048
DEPTH = 4
D_INNER = 2 * D_MODEL
HEAD_DIM = 64
N_HEADS = D_INNER // HEAD_DIM
N_GROUPS = 8
HEADS_PER_GROUP = N_HEADS // N_GROUPS
GROUP_DIM = D_INNER // N_GROUPS
D_STATE = 128
CONV_WIDTH = 5
CHUNK = 128
D_BC = N_GROUPS * D_STATE
D_XBC = D_INNER + 2 * D_BC
POOL_WINDOWS = (2, 4, 8, 16)
POOL_GROUP_DIM = D_MODEL // len(POOL_WINDOWS)
D_FF = 4 * D_MODEL
DEEPNORM_ALPHA = (2.0 * DEPTH) ** 0.25
LN_EPS = 1e-5
RMS_EPS = 1e-5
LOG2_E = 1.4426950408889634

SUBLANES = 8
V7X_VMEM_BYTES = 64 * 1024 * 1024
VMEM_LIMIT = V7X_VMEM_BYTES - 4 * 1024 * 1024

MLP_BM = 1024
MLP_BF = 512
INPROJ_BM = 1024
INPROJ_BN = 1024
OUTPROJ_BM = 1024
OUTPROJ_BK = 1024
CONV_BC = 256
CONV_BR = 128
POOL_BT = 512
SSD_CPS = 16
HALO = SUBLANES
HALO_BF16 = 2 * SUBLANES


def _layer_norm(v, g, b):
    mu = jnp.mean(v, axis=-1, keepdims=True)
    d = v - mu
    var = jnp.mean(d * d, axis=-1, keepdims=True)
    return d * lax.rsqrt(var + LN_EPS) * g + b


def _silu(v):
    return v * (1.0 / (1.0 + jnp.exp2(v * -LOG2_E)))


def _split_bf16(v, parts):
    out = []
    r = v
    for _ in range(parts):
        p = r.astype(BF16)
        out.append(p)
        r = r - p.astype(F32)
    return out


def _store_group_major(o_ref, rows, col0, val):
    gw = o_ref.shape[2]
    step = min(gw, val.shape[1])
    for off in range(0, val.shape[1], step):
        c = col0 + off
        o_ref[c // gw, rows, c % gw:c % gw + step] = val[:, off:off + step].astype(o_ref.dtype)


def _store_group_major_t(o_ref, rows, col0, val):
    gw = o_ref.shape[1]
    for off in range(0, val.shape[1], gw):
        c = col0 + off
        o_ref[c // gw, :, rows] = val[:, off:off + gw].T.astype(o_ref.dtype)


def _group_major_out(m, n_cols, group_width, dtype, index_map):
    shape = jax.ShapeDtypeStruct((n_cols // group_width, m, group_width), dtype)
    spec = pl.BlockSpec((INPROJ_BN // group_width, INPROJ_BM, group_width), index_map)
    return shape, spec


def _matmul_kernel(x_ref, w_ref, o_ref):
    res = jnp.dot(x_ref[...], w_ref[...].astype(BF16), preferred_element_type=F32)
    _store_group_major(o_ref, slice(None), 0, res)


def _in_proj(xb, w_all, layer, col_start, n_cols, group_width):
    m, k = xb.shape
    c0 = col_start // INPROJ_BN
    out_shape, out_spec = _group_major_out(m, n_cols, group_width, F32, lambda j, i: (j, i, 0))
    return pl.pallas_call(
        _matmul_kernel,
        out_shape=out_shape,
        grid=(n_cols // INPROJ_BN, m // INPROJ_BM),
        in_specs=[pl.BlockSpec((INPROJ_BM, k), lambda j, i: (i, 0)),
                  pl.BlockSpec((None, k, INPROJ_BN), lambda j, i: (layer, 0, j + c0))],
        out_specs=out_spec,
        compiler_params=pltpu.CompilerParams(
            dimension_semantics=("parallel", "parallel"), vmem_limit_bytes=VMEM_LIMIT),
        name="in_proj",
    )(xb, w_all)


def _dt_proj_kernel(w_ref, x_ref, o_ref):
    o_ref[0] = lax.dot_general(w_ref[...].astype(BF16), x_ref[...], (((1,), (1,)), ((), ())),
                               preferred_element_type=F32)


def _dt_proj(xb, w_dt_t_all, layer, batch, seq):
    m, k = xb.shape
    nh = w_dt_t_all.shape[1]
    seq_tiles = seq // INPROJ_BM
    return pl.pallas_call(
        _dt_proj_kernel,
        out_shape=jax.ShapeDtypeStruct((batch, nh, seq), F32),
        grid=(m // INPROJ_BM,),
        in_specs=[pl.BlockSpec((None, nh, k), lambda i: (layer, 0, 0)),
                  pl.BlockSpec((INPROJ_BM, k), lambda i: (i, 0))],
        out_specs=pl.BlockSpec((1, nh, INPROJ_BM), lambda i: (i // seq_tiles, 0, i % seq_tiles)),
        compiler_params=pltpu.CompilerParams(
            dimension_semantics=("parallel",), vmem_limit_bytes=VMEM_LIMIT),
        name="dt_proj",
    )(w_dt_t_all, xb)


def _in_proj_conv_kernel(seq_tiles, transposed, xm_ref, xp_ref, xn_ref, w_ref, cw_ref, cb_ref, o_ref,
                         xext_ref):
    bt = xm_ref.shape[0]
    hb = HALO_BF16

    @pl.when(pl.program_id(1) == 0)
    def _():
        pos = pl.program_id(0) % seq_tiles
        xext_ref[0:hb, :] = jnp.where(pos == 0, jnp.zeros_like(xp_ref), xp_ref[...])
        xext_ref[hb:hb + bt, :] = xm_ref[...]
        xext_ref[hb + bt:2 * hb + bt, :] = jnp.where(pos == seq_tiles - 1, jnp.zeros_like(xn_ref),
                                                     xn_ref[...])

    pad = CONV_WIDTH // 2
    rb, halo = CONV_BR, HALO
    for c in range(w_ref.shape[1] // CONV_BC):
        cols = slice(c * CONV_BC, (c + 1) * CONV_BC)
        pre = jnp.dot(xext_ref[...], w_ref[:, cols].astype(BF16), preferred_element_type=F32)
        for r in range(0, bt, rb):
            blk = pre[hb + r - halo:hb + r + rb + halo]
            acc = cb_ref[:, cols] + cw_ref[pad:pad + 1, cols] * blk[halo:halo + rb]
            for k in range(CONV_WIDTH):
                if k != pad:
                    shifted = pltpu.roll(blk, (pad - k) % (rb + 2 * halo), axis=0)[halo:halo + rb]
                    acc = acc + cw_ref[k:k + 1, cols] * shifted
            store = _store_group_major_t if transposed else _store_group_major
            store(o_ref, slice(r, r + rb), c * CONV_BC, _silu(acc))


def _in_proj_conv(xb, w_all, conv_w_all, conv_b_all, layer, seq, col_start, n_cols, group_width,
                  out_dtype, transposed=False):
    m, k = xb.shape
    w0 = (D_INNER + col_start) // INPROJ_BN
    c0 = col_start // INPROJ_BN
    rb = INPROJ_BM // HALO_BF16
    n_halo_blocks = m // HALO_BF16
    if transposed:
        assert group_width == CONV_BR
        out_shape = jax.ShapeDtypeStruct((n_cols // group_width, group_width, m), out_dtype)
        out_spec = pl.BlockSpec((INPROJ_BN // group_width, group_width, INPROJ_BM), lambda i, j: (j, 0, i))
    else:
        out_shape, out_spec = _group_major_out(m, n_cols, group_width, out_dtype, lambda i, j: (j, i, 0))
    return pl.pallas_call(
        functools.partial(_in_proj_conv_kernel, seq // INPROJ_BM, transposed),
        out_shape=out_shape,
        grid=(m // INPROJ_BM, n_cols // INPROJ_BN),
        in_specs=[
            pl.BlockSpec((INPROJ_BM, k), lambda i, j: (i, 0)),
            pl.BlockSpec((HALO_BF16, k), lambda i, j: (jnp.maximum(i * rb - 1, 0), 0)),
            pl.BlockSpec((HALO_BF16, k), lambda i, j: (jnp.minimum((i + 1) * rb, n_halo_blocks - 1), 0)),
            pl.BlockSpec((None, k, INPROJ_BN), lambda i, j: (layer, 0, j + w0)),
            pl.BlockSpec((None, CONV_WIDTH, INPROJ_BN), lambda i, j: (layer, 0, j + c0)),
            pl.BlockSpec((None, 1, INPROJ_BN), lambda i, j: (layer, 0, j + c0)),
        ],
        out_specs=out_spec,
        scratch_shapes=[pltpu.VMEM((INPROJ_BM + 2 * HALO_BF16, k), BF16)],
        compiler_params=pltpu.CompilerParams(
            dimension_semantics=("parallel", "arbitrary"), vmem_limit_bytes=VMEM_LIMIT),
        name="in_proj_conv",
    )(xb, xb, xb, w_all, conv_w_all, conv_b_all)


ROW_CS, ROW_WEND, ROW_WIN = (HEADS_PER_GROUP * r for r in range(3))
TABLE_ROWS = 3 * HEADS_PER_GROUP


def _ssd_block(rev, x_ref, bt_ref, c_ref, z_ref, dt_ref, bias_ref, alog_ref, dskip_ref, nw_ref,
               tri_ref, negmask_ref, ef_ref, o_ref, h_ref, yf_ref):
    hpg = HEADS_PER_GROUP
    cl = SSD_CPS * CHUNK
    cb = pl.program_id(3)
    blk = pl.num_programs(3) - 1 - cb if rev else cb
    g0 = pl.multiple_of(blk * cl, cl)
    tri = tri_ref[int(rev)]
    negmask = negmask_ref[int(rev)]

    raw = dt_ref[0, 0, 0] + bias_ref[0, 0]
    dt = jnp.maximum(raw, 0.0) + jnp.log1p(jnp.exp(-jnp.abs(raw)))
    a = dt * (-LOG2_E * jnp.exp(alog_ref[0, 0]))
    log_dt = jnp.log2(dt)
    zero8 = jnp.zeros((hpg, CHUNK), F32)
    parts = []
    for k in range(SSD_CPS):
        parts += [p.astype(F32) for p in _split_bf16(a[:, k * CHUNK:(k + 1) * CHUNK], 3)] + [zero8]
    cs3 = jnp.dot(jnp.concatenate(parts, axis=0).astype(BF16), tri, preferred_element_type=F32)
    pad = jnp.zeros((CHUNK - TABLE_ROWS, CHUNK), F32)
    row_sub, tab_t = [], []
    for k in range(SSD_CPS):
        c3 = cs3[4 * hpg * k:4 * hpg * (k + 1)]
        cs = c3[0:hpg] + c3[hpg:2 * hpg] + c3[2 * hpg:3 * hpg]
        tot = cs[:, 0:1] if rev else cs[:, CHUNK - 1:CHUNK]
        w_end = jnp.exp2(tot - cs) * dt[:, k * CHUNK:(k + 1) * CHUNK]
        w_in = jnp.exp2(cs)
        tab_t.append(jnp.concatenate([cs, w_end, w_in, pad], axis=0).T)
        row_sub.append(cs - log_dt[:, k * CHUNK:(k + 1) * CHUNK])
    tab_t = jnp.concatenate(tab_t, axis=0)
    t_hi, t_lo = _split_bf16(tab_t, 2)
    fe = jnp.dot(jnp.concatenate([t_hi, t_lo], axis=1), ef_ref[...], preferred_element_type=F32)

    lane = lax.broadcasted_iota(jnp.int32, (CHUNK, GROUP_DIM), 1)
    even_head = (lane % (2 * HEAD_DIM)) < HEAD_DIM
    h = h_ref[...]
    for k in (reversed(range(SSD_CPS)) if rev else range(SSD_CPS)):
        rows = slice(k * CHUNK, (k + 1) * CHUNK)
        x = x_ref[rows, :]
        b_t = bt_ref[:, rows]
        cmat = c_ref[rows, :]
        wend_e = fe[rows, 0:GROUP_DIM]
        win_e = fe[rows, GROUP_DIM:2 * GROUP_DIM]
        scores = jnp.dot(cmat, b_t, preferred_element_type=F32)
        x_even = jnp.where(even_head, x, 0.0).astype(BF16)
        x_odd = jnp.where(even_head, 0.0, x).astype(BF16)
        tab_k = tab_t[rows]
        yd = []
        for j in range(hpg // 2):
            m_pair = []
            for hh in (2 * j, 2 * j + 1):
                seg = (jnp.broadcast_to(tab_k[:, ROW_CS + hh:ROW_CS + hh + 1], (CHUNK, CHUNK))
                       - row_sub[k][hh:hh + 1, :])
                m_pair.append((scores * jnp.exp2(seg + negmask)).astype(BF16))
            cols = slice(2 * j * HEAD_DIM, 2 * (j + 1) * HEAD_DIM)
            rhs = jnp.concatenate([x_even[:, cols], x_odd[:, cols]], axis=0)
            yd.append(jnp.dot(jnp.concatenate(m_pair, axis=1), rhs, preferred_element_type=F32))
        y = (jnp.concatenate(yd, axis=1)
             + jnp.dot(cmat, h.astype(BF16), preferred_element_type=F32) * win_e)
        st = jnp.dot(b_t, (x * wend_e).astype(BF16), preferred_element_type=F32)
        chunk_decay = win_e[0:1, :] if rev else win_e[CHUNK - 1:CHUNK, :]
        h = h * chunk_decay + st

        grows = pl.ds(g0 + k * CHUNK, CHUNK)
        if not rev:
            yf_ref[grows, :] = y
        else:
            ytot = yf_ref[grows, :] + y + x * dskip_ref[...]
            gy = ytot * _silu(z_ref[rows, :])
            ms = jnp.mean(gy * gy, axis=-1, keepdims=True)
            o_ref[rows, :] = (gy * lax.rsqrt(ms + RMS_EPS) * nw_ref[...]).astype(o_ref.dtype)
    h_ref[...] = h


def _ssd_kernel(*refs):
    h_ref = refs[-2]

    @pl.when(pl.program_id(3) == 0)
    def _():
        h_ref[...] = jnp.zeros_like(h_ref)

    @pl.when(pl.program_id(2) == 0)
    def _():
        _ssd_block(False, *refs)

    @pl.when(pl.program_id(2) == 1)
    def _():
        _ssd_block(True, *refs)


def _ssd_constants():
    r = lax.broadcasted_iota(jnp.int32, (CHUNK, CHUNK), 0)
    c = lax.broadcasted_iota(jnp.int32, (CHUNK, CHUNK), 1)
    feeds = jnp.stack([c <= r, c >= r])
    negmask = jnp.where(feeds, 0.0, -jnp.inf).astype(F32)
    tri = jnp.swapaxes(feeds, 1, 2).astype(BF16)
    j = lax.broadcasted_iota(jnp.int32, (CHUNK, 2 * GROUP_DIM), 0)
    n = lax.broadcasted_iota(jnp.int32, (CHUNK, 2 * GROUP_DIM), 1)
    ef = (j == ROW_WEND + HEADS_PER_GROUP * (n // GROUP_DIM) + (n % GROUP_DIM) // HEAD_DIM).astype(BF16)
    return tri, negmask, jnp.concatenate([ef, ef], axis=0)


def _ssd_scan(xs, b_t, cm, z, dt_t, dt_bias_all, a_log_all, d_skip_all, norm_w_all, layer, batch, seq):
    m = xs.shape[1]
    cl = SSD_CPS * CHUNK
    n_cb = seq // cl
    tri, negmask, ef = _ssd_constants()

    def in_blk(b, ph, c):
        return b * n_cb + c + ph * (n_cb - 1 - 2 * c)

    def out_blk(b, ph, c):
        return b * n_cb + n_cb - 1 - ph * c

    head_spec = pl.BlockSpec((None, 1, 1, HEADS_PER_GROUP, 1), lambda b, g, ph, c: (layer, ph, g, 0, 0))
    chan_spec = pl.BlockSpec((None, 1, GROUP_DIM), lambda b, g, ph, c: (layer, 0, g))
    return pl.pallas_call(
        _ssd_kernel,
        out_shape=jax.ShapeDtypeStruct((N_GROUPS, m, GROUP_DIM), BF16),
        grid=(batch, N_GROUPS, 2, n_cb),
        in_specs=[
            pl.BlockSpec((None, cl, GROUP_DIM), lambda b, g, ph, c: (g, in_blk(b, ph, c), 0)),
            pl.BlockSpec((None, D_STATE, cl), lambda b, g, ph, c: (g, 0, in_blk(b, ph, c))),
            pl.BlockSpec((None, cl, D_STATE), lambda b, g, ph, c: (g, in_blk(b, ph, c), 0)),
            pl.BlockSpec((None, cl, GROUP_DIM), lambda b, g, ph, c: (g, out_blk(b, ph, c), 0)),
            pl.BlockSpec((1, 1, 1, HEADS_PER_GROUP, cl),
                         lambda b, g, ph, c: (b, ph, g, 0, c + ph * (n_cb - 1 - 2 * c))),
            head_spec, head_spec, chan_spec, chan_spec,
            pl.BlockSpec(tri.shape, lambda b, g, ph, c: (0, 0, 0)),
            pl.BlockSpec(negmask.shape, lambda b, g, ph, c: (0, 0, 0)),
            pl.BlockSpec(ef.shape, lambda b, g, ph, c: (0, 0)),
        ],
        out_specs=pl.BlockSpec((None, cl, GROUP_DIM), lambda b, g, ph, c: (g, out_blk(b, ph, c), 0)),
        scratch_shapes=[pltpu.VMEM((D_STATE, GROUP_DIM), F32),
                        pltpu.VMEM((seq, GROUP_DIM), F32)],
        compiler_params=pltpu.CompilerParams(
            dimension_semantics=("arbitrary", "arbitrary", "arbitrary", "arbitrary"),
            vmem_limit_bytes=VMEM_LIMIT),
        name="ssd_scan",
    )(xs, b_t, cm, z, dt_t, dt_bias_all, a_log_all, d_skip_all, norm_w_all, tri, negmask, ef)


def _vec_spec(n, layer):
    return pl.BlockSpec((None, 1, n), lambda *_: (layer, 0, 0))


def _residual_norm_inplace(x_ref, g_ref, b_ref, o_ref):
    o_ref[...] = _layer_norm(DEEPNORM_ALPHA * x_ref[...] + o_ref[...], g_ref[...], b_ref[...])


def _outproj_kernel(a_ref, w_ref, x_ref, g_ref, b_ref, o_ref):
    j = pl.program_id(1)

    @pl.when(j == 0)
    def _():
        o_ref[...] = jnp.zeros_like(o_ref)

    a = jnp.concatenate([a_ref[g] for g in range(a_ref.shape[0])], axis=1)
    o_ref[...] += jnp.dot(a, w_ref[...], preferred_element_type=F32)

    @pl.when(j == pl.num_programs(1) - 1)
    def _():
        _residual_norm_inplace(x_ref, g_ref, b_ref, o_ref)


def _out_proj_norm(gy, w_all, x, g_all, b_all, layer, norm_layer):
    n_groups, m, gd = gy.shape
    k = n_groups * gd
    n = w_all.shape[2]
    row = lambda i, j: (i, 0)
    return pl.pallas_call(
        _outproj_kernel,
        out_shape=jax.ShapeDtypeStruct((m, n), F32),
        grid=(m // OUTPROJ_BM, k // OUTPROJ_BK),
        in_specs=[pl.BlockSpec((OUTPROJ_BK // gd, OUTPROJ_BM, gd), lambda i, j: (j, i, 0)),
                  pl.BlockSpec((None, OUTPROJ_BK, n), lambda i, j: (layer, j, 0)),
                  pl.BlockSpec((OUTPROJ_BM, n), row),
                  _vec_spec(n, norm_layer), _vec_spec(n, norm_layer)],
        out_specs=pl.BlockSpec((OUTPROJ_BM, n), row),
        compiler_params=pltpu.CompilerParams(
            dimension_semantics=("parallel", "arbitrary"), vmem_limit_bytes=VMEM_LIMIT),
        name="out_proj_norm",
    )(gy, w_all, x, g_all, b_all)


def _pool_kernel(seq, x_ref, p_ref, n_ref, w_ref, bias_ref, scale_ref, g_ref, b_ref,
                 o_ref, v_ref):
    bt = x_ref.shape[0]
    n_ext = bt + 2 * HALO
    seq_tiles = seq // bt
    pos = pl.program_id(0) % seq_tiles
    t = pos * bt + lax.broadcasted_iota(jnp.int32, (bt, 1), 0)
    gd = POOL_GROUP_DIM
    shift_down = lambda v, s: pltpu.roll(v, s % n_ext, axis=0)
    for gi, win in enumerate(POOL_WINDOWS):
        cols = slice(gi * gd, (gi + 1) * gd)
        start = t - win // 2
        cnt = (jnp.clip(start + win, 0, seq) - jnp.clip(start, 0, seq)).astype(F32)
        ext = jnp.concatenate([jnp.where(pos == 0, 0.0, p_ref[:, cols]), x_ref[:, cols],
                               jnp.where(pos == seq_tiles - 1, 0.0, n_ref[:, cols])], axis=0)
        psum, span = ext, 1
        while span < win:
            psum = psum + shift_down(psum, span)
            span *= 2
        ahead = win - win // 2 - 1
        wsum = (shift_down(psum, -ahead) if ahead else psum)[HALO:HALO + bt]
        xg = x_ref[:, cols]
        mg = (wsum / cnt - xg).astype(BF16)
        y = jnp.dot(mg, w_ref[gi].astype(BF16), preferred_element_type=F32) + bias_ref[:, cols]
        v_ref[:, cols] = DEEPNORM_ALPHA * xg + y * scale_ref[:, cols]
    o_ref[...] = _layer_norm(v_ref[...], g_ref[...], b_ref[...])


def _pool_mixer_norm(x, w_all, bias_all, scale_all, g_all, b_all, layer, norm_layer, seq):
    m, d = x.shape
    rb = POOL_BT // HALO
    n_halo_blocks = m // HALO
    row = lambda i: (i, 0)
    return pl.pallas_call(
        functools.partial(_pool_kernel, seq),
        out_shape=jax.ShapeDtypeStruct((m, d), F32),
        grid=(m // POOL_BT,),
        in_specs=[pl.BlockSpec((POOL_BT, d), row),
                  pl.BlockSpec((HALO, d), lambda i: (jnp.maximum(i * rb - 1, 0), 0)),
                  pl.BlockSpec((HALO, d), lambda i: (jnp.minimum((i + 1) * rb, n_halo_blocks - 1), 0)),
                  pl.BlockSpec((None,) + w_all.shape[1:], lambda i: (layer, 0, 0, 0)),
                  _vec_spec(d, layer), _vec_spec(d, layer),
                  _vec_spec(d, norm_layer), _vec_spec(d, norm_layer)],
        out_specs=pl.BlockSpec((POOL_BT, d), row),
        scratch_shapes=[pltpu.VMEM((POOL_BT, d), F32)],
        compiler_params=pltpu.CompilerParams(
            dimension_semantics=("parallel",), vmem_limit_bytes=VMEM_LIMIT),
        name="pool_mixer_norm",
    )(x, x, x, w_all, bias_all, scale_all, g_all, b_all)


def _mlp_kernel(x_ref, w1_ref, w2_ref, g_ref, b_ref, o_ref, xb_ref):
    j = pl.program_id(1)

    @pl.when(j == 0)
    def _():
        xb_ref[...] = x_ref[...].astype(BF16)
        o_ref[...] = jnp.zeros_like(o_ref)

    hid = jnp.dot(xb_ref[...], w1_ref[...], preferred_element_type=F32)
    hid = jnp.square(jnp.maximum(hid, 0.0)).astype(BF16)
    o_ref[...] += jnp.dot(hid, w2_ref[...].astype(BF16), preferred_element_type=F32)

    @pl.when(j == pl.num_programs(1) - 1)
    def _():
        _residual_norm_inplace(x_ref, g_ref, b_ref, o_ref)


def _mlp_norm(x, w1_all, w2_all, g_all, b_all, layer):
    m, d = x.shape
    f = w2_all.shape[1]
    row = lambda i, j: (i, 0)
    return pl.pallas_call(
        _mlp_kernel,
        out_shape=jax.ShapeDtypeStruct((m, d), F32),
        grid=(m // MLP_BM, f // MLP_BF),
        in_specs=[pl.BlockSpec((MLP_BM, d), row),
                  pl.BlockSpec((None, d, MLP_BF), lambda i, j: (layer, 0, j)),
                  pl.BlockSpec((None, MLP_BF, d), lambda i, j: (layer, j, 0)),
                  _vec_spec(d, layer), _vec_spec(d, layer)],
        out_specs=pl.BlockSpec((MLP_BM, d), row),
        scratch_shapes=[pltpu.VMEM((MLP_BM, d), BF16)],
        compiler_params=pltpu.CompilerParams(
            dimension_semantics=("parallel", "arbitrary"), vmem_limit_bytes=VMEM_LIMIT),
        name="mlp_norm",
    )(x, w1_all, w2_all, g_all, b_all)


def kernel(x, ssd_in_proj, ssd_conv_w, ssd_conv_b, ssd_dt_bias, ssd_A_log, ssd_D, ssd_norm_w,
           ssd_out_proj, pool_w, pool_b, pool_scale, mlp_w1, mlp_w2, ln_mix_g, ln_mix_b,
           ln_ffn_g, ln_ffn_b):
    batch, seq, d = x.shape
    m = batch * seq
    assert d == D_MODEL and seq % (SSD_CPS * CHUNK) == 0 and seq % POOL_BT == 0 and seq % INPROJ_BM == 0
    n_ssd, n_pool = ssd_in_proj.shape[0], pool_w.shape[0]
    vecs = lambda v: v.reshape(v.shape[0], 1, -1)
    w_dt_t = jnp.swapaxes(ssd_in_proj[:, :, D_INNER + D_XBC:], 1, 2)
    w_out = ssd_out_proj.astype(BF16)
    w1 = mlp_w1.astype(BF16)
    conv_w = ssd_conv_w.reshape(n_ssd, CONV_WIDTH, D_XBC)
    conv_b = vecs(ssd_conv_b)
    head_shape = (n_ssd, 2, N_GROUPS, HEADS_PER_GROUP, 1)
    dt_bias = ssd_dt_bias.reshape(head_shape)
    a_log = ssd_A_log.reshape(head_shape)
    d_skip = vecs(jnp.repeat(ssd_D, HEAD_DIM, axis=-1))
    norm_w = vecs(ssd_norm_w)
    pool_b, pool_scale = pool_b.reshape(n_pool, 1, d), vecs(pool_scale)
    ln_mix_g, ln_mix_b, ln_ffn_g, ln_ffn_b = (vecs(v) for v in (ln_mix_g, ln_mix_b, ln_ffn_g, ln_ffn_b))

    xf = x.reshape(m, d)
    for i in range(DEPTH):
        j = i // 2
        if i % 2 == 0:
            xb = xf.astype(BF16)
            z = _in_proj(xb, ssd_in_proj, j, 0, D_INNER, GROUP_DIM)
            xs = _in_proj_conv(xb, ssd_in_proj, conv_w, conv_b, j, seq, 0, D_INNER, GROUP_DIM, F32)
            b_t = _in_proj_conv(xb, ssd_in_proj, conv_w, conv_b, j, seq, D_INNER, D_BC, D_STATE, BF16,
                                transposed=True)
            cm = _in_proj_conv(xb, ssd_in_proj, conv_w, conv_b, j, seq, D_INNER + D_BC, D_BC, D_STATE, BF16)
            dt_t = _dt_proj(xb, w_dt_t, j, batch, seq).reshape(batch, 2, N_GROUPS, HEADS_PER_GROUP, seq)
            gy = _ssd_scan(xs, b_t, cm, z, dt_t, dt_bias, a_log, d_skip, norm_w, j, batch, seq)
            xf = _out_proj_norm(gy, w_out, xf, ln_mix_g, ln_mix_b, j, i)
        else:
            xf = _pool_mixer_norm(xf, pool_w, pool_b, pool_scale, ln_mix_g, ln_mix_b, j, i, seq)
        xf = _mlp_norm(xf, w1, mlp_w2, ln_ffn_g, ln_ffn_b, i)
    return xf.reshape(batch, seq, d)
```

```python
import functools

import jax
import jax.numpy as jnp
from jax import lax
from jax.experimental import pallas as pl
from jax.experimental.pallas import tpu as pltpu

F32 = jnp.float32
BF16 = jnp.bfloat16

D_MODEL = 2048
DEPTH = 4
D_INNER = 2 * D_MODEL
HEAD_DIM = 64
N_HEADS = D_INNER // HEAD_DIM
N_GROUPS = 8
HEADS_PER_GROUP = N_HEADS // N_GROUPS
GROUP_DIM = D_INNER // N_GROUPS
D_STATE = 128
CONV_WIDTH = 5
CHUNK = 128
D_BC = N_GROUPS * D_STATE
D_XBC = D_INNER + 2 * D_BC
POOL_WINDOWS = (2, 4, 8, 16)
POOL_GROUP_DIM = D_MODEL // len(POOL_WINDOWS)
D_FF = 4 * D_MODEL
DEEPNORM_ALPHA = (2.0 * DEPTH) ** 0.25
LN_EPS = 1e-5
RMS_EPS = 1e-5
LOG2_E = 1.4426950408889634

SUBLANES = 8
V7X_VMEM_BYTES = 64 * 1024 * 1024
VMEM_LIMIT = V7X_VMEM_BYTES - 4 * 1024 * 1024

MLP_BM = 1024
MLP_BF = 512
INPROJ_BM = 1024
INPROJ_BN = 1024
OUTPROJ_BM = 1024
OUTPROJ_BK = 1024
CONV_BC = 256
CONV_BR = 128
POOL_BT = 512
SSD_CPS = 16
HALO = SUBLANES
HALO_BF16 = 2 * SUBLANES


def _layer_norm(v, g, b):
    mu = jnp.mean(v, axis=-1, keepdims=True)
    d = v - mu
    var = jnp.mean(d * d, axis=-1, keepdims=True)
    return d * lax.rsqrt(var + LN_EPS) * g + b


def _silu(v):
    return v * (1.0 / (1.0 + jnp.exp2(v * -LOG2_E)))


def _split_bf16(v, parts):
    out = []
    r = v
    for _ in range(parts):
        p = r.astype(BF16)
        out.append(p)
        r = r - p.astype(F32)
    return out


def _store_group_major(o_ref, rows, col0, val):
    gw = o_ref.shape[2]
    step = min(gw, val.shape[1])
    for off in range(0, val.shape[1], step):
        c = col0 + off
        o_ref[c // gw, rows, c % gw:c % gw + step] = val[:, off:off + step].astype(o_ref.dtype)


def _store_group_major_t(o_ref, rows, col0, val):
    gw = o_ref.shape[1]
    for off in range(0, val.shape[1], gw):
        c = col0 + off
        o_ref[c // gw, :, rows] = val[:, off:off + gw].T.astype(o_ref.dtype)


def _group_major_out(m, n_cols, group_width, dtype, index_map):
    shape = jax.ShapeDtypeStruct((n_cols // group_width, m, group_width), dtype)
    spec = pl.BlockSpec((INPROJ_BN // group_width, INPROJ_BM, group_width), index_map)
    return shape, spec


def _matmul_kernel(x_ref, w_ref, o_ref):
    res = jnp.dot(x_ref[...], w_ref[...].astype(BF16), preferred_element_type=F32)
    _store_group_major(o_ref, slice(None), 0, res)


def _in_proj(xb, w_all, layer, col_start, n_cols, group_width):
    m, k = xb.shape
    c0 = col_start // INPROJ_BN
    out_shape, out_spec = _group_major_out(m, n_cols, group_width, F32, lambda j, i: (j, i, 0))
    return pl.pallas_call(
        _matmul_kernel,
        out_shape=out_shape,
        grid=(n_cols // INPROJ_BN, m // INPROJ_BM),
        in_specs=[pl.BlockSpec((INPROJ_BM, k), lambda j, i: (i, 0)),
                  pl.BlockSpec((None, k, INPROJ_BN), lambda j, i: (layer, 0, j + c0))],
        out_specs=out_spec,
        compiler_params=pltpu.CompilerParams(
            dimension_semantics=("parallel", "parallel"), vmem_limit_bytes=VMEM_LIMIT),
        name="in_proj",
    )(xb, w_all)


def _dt_proj_kernel(w_ref, x_ref, o_ref, xb_ref):
    xb = x_ref[...].astype(BF16)
    xb_ref[...] = xb
    o_ref[0] = lax.dot_general(w_ref[...].astype(BF16), xb, (((1,), (1,)), ((), ())),
                               preferred_element_type=F32)


def _dt_proj(x, w_dt_t_all, layer, batch, seq):
    m, k = x.shape
    nh = w_dt_t_all.shape[1]
    seq_tiles = seq // INPROJ_BM
    return pl.pallas_call(
        _dt_proj_kernel,
        out_shape=(jax.ShapeDtypeStruct((batch, nh, seq), F32), jax.ShapeDtypeStruct((m, k), BF16)),
        grid=(m // INPROJ_BM,),
        in_specs=[pl.BlockSpec((None, nh, k), lambda i: (layer, 0, 0)),
                  pl.BlockSpec((INPROJ_BM, k), lambda i: (i, 0))],
        out_specs=(pl.BlockSpec((1, nh, INPROJ_BM), lambda i: (i // seq_tiles, 0, i % seq_tiles)),
                   pl.BlockSpec((INPROJ_BM, k), lambda i: (i, 0))),
        compiler_params=pltpu.CompilerParams(
            dimension_semantics=("parallel",), vmem_limit_bytes=VMEM_LIMIT),
        name="dt_proj",
    )(w_dt_t_all, x)


def _in_proj_conv_kernel(seq_tiles, transposed, xm_ref, xp_ref, xn_ref, w_ref, cw_ref, cb_ref, o_ref,
                         xext_ref):
    bt = xm_ref.shape[0]
    hb = HALO_BF16

    @pl.when(pl.program_id(1) == 0)
    def _():
        pos = pl.program_id(0) % seq_tiles
        xext_ref[0:hb, :] = jnp.where(pos == 0, jnp.zeros_like(xp_ref), xp_ref[...])
        xext_ref[hb:hb + bt, :] = xm_ref[...]
        xext_ref[hb + bt:2 * hb + bt, :] = jnp.where(pos == seq_tiles - 1, jnp.zeros_like(xn_ref),
                                                     xn_ref[...])

    pad = CONV_WIDTH // 2
    rb, halo = CONV_BR, HALO
    for c in range(w_ref.shape[1] // CONV_BC):
        cols = slice(c * CONV_BC, (c + 1) * CONV_BC)
        pre = jnp.dot(xext_ref[...], w_ref[:, cols].astype(BF16), preferred_element_type=F32)
        for r in range(0, bt, rb):
            blk = pre[hb + r - halo:hb + r + rb + halo]
            acc = cb_ref[:, cols] + cw_ref[pad:pad + 1, cols] * blk[halo:halo + rb]
            for k in range(CONV_WIDTH):
                if k != pad:
                    shifted = pltpu.roll(blk, (pad - k) % (rb + 2 * halo), axis=0)[halo:halo + rb]
                    acc = acc + cw_ref[k:k + 1, cols] * shifted
            store = _store_group_major_t if transposed else _store_group_major
            store(o_ref, slice(r, r + rb), c * CONV_BC, _silu(acc))


def _in_proj_conv(xb, w_all, conv_w_all, conv_b_all, layer, seq, col_start, n_cols, group_width,
                  out_dtype, transposed=False):
    m, k = xb.shape
    w0 = (D_INNER + col_start) // INPROJ_BN
    c0 = col_start // INPROJ_BN
    rb = INPROJ_BM // HALO_BF16
    n_halo_blocks = m // HALO_BF16
    if transposed:
        assert group_width == CONV_BR
        out_shape = jax.ShapeDtypeStruct((n_cols // group_width, group_width, m), out_dtype)
        out_spec = pl.BlockSpec((INPROJ_BN // group_width, group_width, INPROJ_BM), lambda i, j: (j, 0, i))
    else:
        out_shape, out_spec = _group_major_out(m, n_cols, group_width, out_dtype, lambda i, j: (j, i, 0))
    return pl.pallas_call(
        functools.partial(_in_proj_conv_kernel, seq // INPROJ_BM, transposed),
        out_shape=out_shape,
        grid=(m // INPROJ_BM, n_cols // INPROJ_BN),
        in_specs=[
            pl.BlockSpec((INPROJ_BM, k), lambda i, j: (i, 0)),
            pl.BlockSpec((HALO_BF16, k), lambda i, j: (jnp.maximum(i * rb - 1, 0), 0)),
            pl.BlockSpec((HALO_BF16, k), lambda i, j: (jnp.minimum((i + 1) * rb, n_halo_blocks - 1), 0)),
            pl.BlockSpec((None, k, INPROJ_BN), lambda i, j: (layer, 0, j + w0)),
            pl.BlockSpec((None, CONV_WIDTH, INPROJ_BN), lambda i, j: (layer, 0, j + c0)),
            pl.BlockSpec((None, 1, INPROJ_BN), lambda i, j: (layer, 0, j + c0)),
        ],
        out_specs=out_spec,
        scratch_shapes=[pltpu.VMEM((INPROJ_BM + 2 * HALO_BF16, k), BF16)],
        compiler_params=pltpu.CompilerParams(
            dimension_semantics=("parallel", "arbitrary"), vmem_limit_bytes=VMEM_LIMIT),
        name="in_proj_conv",
    )(xb, xb, xb, w_all, conv_w_all, conv_b_all)


ROW_CS, ROW_WEND, ROW_WIN = (HEADS_PER_GROUP * r for r in range(3))
TABLE_ROWS = 3 * HEADS_PER_GROUP


def _ssd_block(rev, x_ref, bt_ref, c_ref, z_ref, dt_ref, bias_ref, alog_ref, dskip_ref, nw_ref,
               tri_ref, negmask_ref, ef_ref, o_ref, h_ref, yf_ref):
    hpg = HEADS_PER_GROUP
    cl = SSD_CPS * CHUNK
    cb = pl.program_id(3)
    blk = pl.num_programs(3) - 1 - cb if rev else cb
    g0 = pl.multiple_of(blk * cl, cl)
    tri = tri_ref[int(rev)]
    negmask = negmask_ref[int(rev)]

    raw = dt_ref[0, 0, 0] + bias_ref[0, 0]
    dt = jnp.maximum(raw, 0.0) + jnp.log1p(jnp.exp(-jnp.abs(raw)))
    a = dt * (-LOG2_E * jnp.exp(alog_ref[0, 0]))
    log_dt = jnp.log2(dt)
    zero8 = jnp.zeros((hpg, CHUNK), F32)
    parts = []
    for k in range(SSD_CPS):
        parts += [p.astype(F32) for p in _split_bf16(a[:, k * CHUNK:(k + 1) * CHUNK], 3)] + [zero8]
    cs3 = jnp.dot(jnp.concatenate(parts, axis=0).astype(BF16), tri, preferred_element_type=F32)
    pad = jnp.zeros((CHUNK - TABLE_ROWS, CHUNK), F32)
    row_sub, tab_t = [], []
    for k in range(SSD_CPS):
        c3 = cs3[4 * hpg * k:4 * hpg * (k + 1)]
        cs = c3[0:hpg] + c3[hpg:2 * hpg] + c3[2 * hpg:3 * hpg]
        tot = cs[:, 0:1] if rev else cs[:, CHUNK - 1:CHUNK]
        w_end = jnp.exp2(tot - cs) * dt[:, k * CHUNK:(k + 1) * CHUNK]
        w_in = jnp.exp2(cs)
        tab_t.append(jnp.concatenate([cs, w_end, w_in, pad], axis=0).T)
        row_sub.append(cs - log_dt[:, k * CHUNK:(k + 1) * CHUNK])
    tab_t = jnp.concatenate(tab_t, axis=0)
    t_hi, t_lo = _split_bf16(tab_t, 2)
    fe = jnp.dot(jnp.concatenate([t_hi, t_lo], axis=1), ef_ref[...], preferred_element_type=F32)

    lane = lax.broadcasted_iota(jnp.int32, (CHUNK, GROUP_DIM), 1)
    even_head = (lane % (2 * HEAD_DIM)) < HEAD_DIM
    h = h_ref[...]
    for k in (reversed(range(SSD_CPS)) if rev else range(SSD_CPS)):
        rows = slice(k * CHUNK, (k + 1) * CHUNK)
        x = x_ref[rows, :]
        b_t = bt_ref[:, rows]
        cmat = c_ref[rows, :]
        wend_e = fe[rows, 0:GROUP_DIM]
        win_e = fe[rows, GROUP_DIM:2 * GROUP_DIM]
        scores = jnp.dot(cmat, b_t, preferred_element_type=F32)
        x_even = jnp.where(even_head, x, 0.0).astype(BF16)
        x_odd = jnp.where(even_head, 0.0, x).astype(BF16)
        tab_k = tab_t[rows]
        yd = []
        for j in range(hpg // 2):
            m_pair = []
            for hh in (2 * j, 2 * j + 1):
                seg = (jnp.broadcast_to(tab_k[:, ROW_CS + hh:ROW_CS + hh + 1], (CHUNK, CHUNK))
                       - row_sub[k][hh:hh + 1, :])
                m_pair.append((scores * jnp.exp2(seg + negmask)).astype(BF16))
            cols = slice(2 * j * HEAD_DIM, 2 * (j + 1) * HEAD_DIM)
            rhs = jnp.concatenate([x_even[:, cols], x_odd[:, cols]], axis=0)
            yd.append(jnp.dot(jnp.concatenate(m_pair, axis=1), rhs, preferred_element_type=F32))
        y = (jnp.concatenate(yd, axis=1)
             + jnp.dot(cmat, h.astype(BF16), preferred_element_type=F32) * win_e)
        st = jnp.dot(b_t, (x * wend_e).astype(BF16), preferred_element_type=F32)
        chunk_decay = win_e[0:1, :] if rev else win_e[CHUNK - 1:CHUNK, :]
        h = h * chunk_decay + st

        grows = pl.ds(g0 + k * CHUNK, CHUNK)
        if not rev:
            yf_ref[grows, :] = y
        else:
            ytot = yf_ref[grows, :] + y + x * dskip_ref[...]
            gy = ytot * _silu(z_ref[rows, :])
            ms = jnp.mean(gy * gy, axis=-1, keepdims=True)
            o_ref[rows, :] = (gy * lax.rsqrt(ms + RMS_EPS) * nw_ref[...]).astype(o_ref.dtype)
    h_ref[...] = h


def _ssd_kernel(*refs):
    h_ref = refs[-2]

    @pl.when(pl.program_id(3) == 0)
    def _():
        h_ref[...] = jnp.zeros_like(h_ref)

    @pl.when(pl.program_id(2) == 0)
    def _():
        _ssd_block(False, *refs)

    @pl.when(pl.program_id(2) == 1)
    def _():
        _ssd_block(True, *refs)


def _ssd_constants():
    r = lax.broadcasted_iota(jnp.int32, (CHUNK, CHUNK), 0)
    c = lax.broadcasted_iota(jnp.int32, (CHUNK, CHUNK), 1)
    feeds = jnp.stack([c <= r, c >= r])
    negmask = jnp.where(feeds, 0.0, -jnp.inf).astype(F32)
    tri = jnp.swapaxes(feeds, 1, 2).astype(BF16)
    j = lax.broadcasted_iota(jnp.int32, (CHUNK, 2 * GROUP_DIM), 0)
    n = lax.broadcasted_iota(jnp.int32, (CHUNK, 2 * GROUP_DIM), 1)
    ef = (j == ROW_WEND + HEADS_PER_GROUP * (n // GROUP_DIM) + (n % GROUP_DIM) // HEAD_DIM).astype(BF16)
    return tri, negmask, jnp.concatenate([ef, ef], axis=0)


def _ssd_scan(xs, b_t, cm, z, dt_t, dt_bias_all, a_log_all, d_skip_all, norm_w_all, layer, batch, seq):
    m = xs.shape[1]
    cl = SSD_CPS * CHUNK
    n_cb = seq // cl
    tri, negmask, ef = _ssd_constants()

    def in_blk(b, ph, c):
        return b * n_cb + c + ph * (n_cb - 1 - 2 * c)

    def out_blk(b, ph, c):
        return b * n_cb + n_cb - 1 - ph * c

    head_spec = pl.BlockSpec((None, 1, 1, HEADS_PER_GROUP, 1), lambda b, g, ph, c: (layer, ph, g, 0, 0))
    chan_spec = pl.BlockSpec((None, 1, GROUP_DIM), lambda b, g, ph, c: (layer, 0, g))
    return pl.pallas_call(
        _ssd_kernel,
        out_shape=jax.ShapeDtypeStruct((N_GROUPS, m, GROUP_DIM), BF16),
        grid=(batch, N_GROUPS, 2, n_cb),
        in_specs=[
            pl.BlockSpec((None, cl, GROUP_DIM), lambda b, g, ph, c: (g, in_blk(b, ph, c), 0)),
            pl.BlockSpec((None, D_STATE, cl), lambda b, g, ph, c: (g, 0, in_blk(b, ph, c))),
            pl.BlockSpec((None, cl, D_STATE), lambda b, g, ph, c: (g, in_blk(b, ph, c), 0)),
            pl.BlockSpec((None, cl, GROUP_DIM), lambda b, g, ph, c: (g, out_blk(b, ph, c), 0)),
            pl.BlockSpec((1, 1, 1, HEADS_PER_GROUP, cl),
                         lambda b, g, ph, c: (b, ph, g, 0, c + ph * (n_cb - 1 - 2 * c))),
            head_spec, head_spec, chan_spec, chan_spec,
            pl.BlockSpec(tri.shape, lambda b, g, ph, c: (0, 0, 0)),
            pl.BlockSpec(negmask.shape, lambda b, g, ph, c: (0, 0, 0)),
            pl.BlockSpec(ef.shape, lambda b, g, ph, c: (0, 0)),
        ],
        out_specs=pl.BlockSpec((None, cl, GROUP_DIM), lambda b, g, ph, c: (g, out_blk(b, ph, c), 0)),
        scratch_shapes=[pltpu.VMEM((D_STATE, GROUP_DIM), F32),
                        pltpu.VMEM((seq, GROUP_DIM), F32)],
        compiler_params=pltpu.CompilerParams(
            dimension_semantics=("arbitrary", "arbitrary", "arbitrary", "arbitrary"),
            vmem_limit_bytes=VMEM_LIMIT),
        name="ssd_scan",
    )(xs, b_t, cm, z, dt_t, dt_bias_all, a_log_all, d_skip_all, norm_w_all, tri, negmask, ef)


def _vec_spec(n, layer):
    return pl.BlockSpec((None, 1, n), lambda *_: (layer, 0, 0))


def _residual_norm_inplace(x_ref, g_ref, b_ref, o_ref):
    o_ref[...] = _layer_norm(DEEPNORM_ALPHA * x_ref[...] + o_ref[...], g_ref[...], b_ref[...])


def _outproj_kernel(a_ref, w_ref, x_ref, g_ref, b_ref, o_ref):
    j = pl.program_id(1)

    @pl.when(j == 0)
    def _():
        o_ref[...] = jnp.zeros_like(o_ref)

    a = jnp.concatenate([a_ref[g] for g in range(a_ref.shape[0])], axis=1)
    o_ref[...] += jnp.dot(a, w_ref[...], preferred_element_type=F32)

    @pl.when(j == pl.num_programs(1) - 1)
    def _():
        _residual_norm_inplace(x_ref, g_ref, b_ref, o_ref)


def _out_proj_norm(gy, w_all, x, g_all, b_all, layer, norm_layer):
    n_groups, m, gd = gy.shape
    k = n_groups * gd
    n = w_all.shape[2]
    row = lambda i, j: (i, 0)
    return pl.pallas_call(
        _outproj_kernel,
        out_shape=jax.ShapeDtypeStruct((m, n), F32),
        grid=(m // OUTPROJ_BM, k // OUTPROJ_BK),
        in_specs=[pl.BlockSpec((OUTPROJ_BK // gd, OUTPROJ_BM, gd), lambda i, j: (j, i, 0)),
                  pl.BlockSpec((None, OUTPROJ_BK, n), lambda i, j: (layer, j, 0)),
                  pl.BlockSpec((OUTPROJ_BM, n), row),
                  _vec_spec(n, norm_layer), _vec_spec(n, norm_layer)],
        out_specs=pl.BlockSpec((OUTPROJ_BM, n), row),
        compiler_params=pltpu.CompilerParams(
            dimension_semantics=("parallel", "arbitrary"), vmem_limit_bytes=VMEM_LIMIT),
        name="out_proj_norm",
    )(gy, w_all, x, g_all, b_all)


def _pool_kernel(seq, x_ref, p_ref, n_ref, w_ref, bias_ref, scale_ref, g_ref, b_ref,
                 o_ref, v_ref):
    bt = x_ref.shape[0]
    n_ext = bt + 2 * HALO
    seq_tiles = seq // bt
    pos = pl.program_id(0) % seq_tiles
    t = pos * bt + lax.broadcasted_iota(jnp.int32, (bt, 1), 0)
    gd = POOL_GROUP_DIM
    shift_down = lambda v, s: pltpu.roll(v, s % n_ext, axis=0)
    for gi, win in enumerate(POOL_WINDOWS):
        cols = slice(gi * gd, (gi + 1) * gd)
        start = t - win // 2
        cnt = (jnp.clip(start + win, 0, seq) - jnp.clip(start, 0, seq)).astype(F32)
        ext = jnp.concatenate([jnp.where(pos == 0, 0.0, p_ref[:, cols]), x_ref[:, cols],
                               jnp.where(pos == seq_tiles - 1, 0.0, n_ref[:, cols])], axis=0)
        psum, span = ext, 1
        while span < win:
            psum = psum + shift_down(psum, span)
            span *= 2
        ahead = win - win // 2 - 1
        wsum = (shift_down(psum, -ahead) if ahead else psum)[HALO:HALO + bt]
        xg = x_ref[:, cols]
        mg = (wsum / cnt - xg).astype(BF16)
        y = jnp.dot(mg, w_ref[gi].astype(BF16), preferred_element_type=F32) + bias_ref[:, cols]
        v_ref[:, cols] = DEEPNORM_ALPHA * xg + y * scale_ref[:, cols]
    o_ref[...] = _layer_norm(v_ref[...], g_ref[...], b_ref[...])


def _pool_mixer_norm(x, w_all, bias_all, scale_all, g_all, b_all, layer, norm_layer, seq):
    m, d = x.shape
    rb = POOL_BT // HALO
    n_halo_blocks = m // HALO
    row = lambda i: (i, 0)
    return pl.pallas_call(
        functools.partial(_pool_kernel, seq),
        out_shape=jax.ShapeDtypeStruct((m, d), F32),
        grid=(m // POOL_BT,),
        in_specs=[pl.BlockSpec((POOL_BT, d), row),
                  pl.BlockSpec((HALO, d), lambda i: (jnp.maximum(i * rb - 1, 0), 0)),
                  pl.BlockSpec((HALO, d), lambda i: (jnp.minimum((i + 1) * rb, n_halo_blocks - 1), 0)),
                  pl.BlockSpec((None,) + w_all.shape[1:], lambda i: (layer, 0, 0, 0)),
                  _vec_spec(d, layer), _vec_spec(d, layer),
                  _vec_spec(d, norm_layer), _vec_spec(d, norm_layer)],
        out_specs=pl.BlockSpec((POOL_BT, d), row),
        scratch_shapes=[pltpu.VMEM((POOL_BT, d), F32)],
        compiler_params=pltpu.CompilerParams(
            dimension_semantics=("parallel",), vmem_limit_bytes=VMEM_LIMIT),
        name="pool_mixer_norm",
    )(x, x, x, w_all, bias_all, scale_all, g_all, b_all)


def _mlp_kernel(x_ref, w1_ref, w2_ref, g_ref, b_ref, o_ref, xb_ref):
    j = pl.program_id(1)

    @pl.when(j == 0)
    def _():
        xb_ref[...] = x_ref[...].astype(BF16)
        o_ref[...] = jnp.zeros_like(o_ref)

    hid = jnp.dot(xb_ref[...], w1_ref[...], preferred_element_type=F32)
    hid = jnp.square(jnp.maximum(hid, 0.0)).astype(BF16)
    o_ref[...] += jnp.dot(hid, w2_ref[...].astype(BF16), preferred_element_type=F32)

    @pl.when(j == pl.num_programs(1) - 1)
    def _():
        _residual_norm_inplace(x_ref, g_ref, b_ref, o_ref)


def _mlp_norm(x, w1_all, w2_all, g_all, b_all, layer):
    m, d = x.shape
    f = w2_all.shape[1]
    row = lambda i, j: (i, 0)
    return pl.pallas_call(
        _mlp_kernel,
        out_shape=jax.ShapeDtypeStruct((m, d), F32),
        grid=(m // MLP_BM, f // MLP_BF),
        in_specs=[pl.BlockSpec((MLP_BM, d), row),
                  pl.BlockSpec((None, d, MLP_BF), lambda i, j: (layer, 0, j)),
                  pl.BlockSpec((None, MLP_BF, d), lambda i, j: (layer, j, 0)),
                  _vec_spec(d, layer), _vec_spec(d, layer)],
        out_specs=pl.BlockSpec((MLP_BM, d), row),
        scratch_shapes=[pltpu.VMEM((MLP_BM, d), BF16)],
        compiler_params=pltpu.CompilerParams(
            dimension_semantics=("parallel", "arbitrary"), vmem_limit_bytes=VMEM_LIMIT),
        name="mlp_norm",
    )(x, w1_all, w2_all, g_all, b_all)


def kernel(x, ssd_in_proj, ssd_conv_w, ssd_conv_b, ssd_dt_bias, ssd_A_log, ssd_D, ssd_norm_w,
           ssd_out_proj, pool_w, pool_b, pool_scale, mlp_w1, mlp_w2, ln_mix_g, ln_mix_b,
           ln_ffn_g, ln_ffn_b):
    batch, seq, d = x.shape
    m = batch * seq
    assert d == D_MODEL and seq % (SSD_CPS * CHUNK) == 0 and seq % POOL_BT == 0 and seq % INPROJ_BM == 0
    n_ssd, n_pool = ssd_in_proj.shape[0], pool_w.shape[0]
    vecs = lambda v: v.reshape(v.shape[0], 1, -1)
    w_dt_t = jnp.swapaxes(ssd_in_proj[:, :, D_INNER + D_XBC:], 1, 2)
    w_out = ssd_out_proj.astype(BF16)
    w1 = mlp_w1.astype(BF16)
    conv_w = ssd_conv_w.reshape(n_ssd, CONV_WIDTH, D_XBC)
    conv_b = vecs(ssd_conv_b)
    head_shape = (n_ssd, 2, N_GROUPS, HEADS_PER_GROUP, 1)
    dt_bias = ssd_dt_bias.reshape(head_shape)
    a_log = ssd_A_log.reshape(head_shape)
    d_skip = vecs(jnp.repeat(ssd_D, HEAD_DIM, axis=-1))
    norm_w = vecs(ssd_norm_w)
    pool_b, pool_scale = pool_b.reshape(n_pool, 1, d), vecs(pool_scale)
    ln_mix_g, ln_mix_b, ln_ffn_g, ln_ffn_b = (vecs(v) for v in (ln_mix_g, ln_mix_b, ln_ffn_g, ln_ffn_b))

    xf = x.reshape(m, d)
    for i in range(DEPTH):
        j = i // 2
        if i % 2 == 0:
            dt_raw, xb = _dt_proj(xf, w_dt_t, j, batch, seq)
            dt_t = dt_raw.reshape(batch, 2, N_GROUPS, HEADS_PER_GROUP, seq)
            z = _in_proj(xb, ssd_in_proj, j, 0, D_INNER, GROUP_DIM)
            xs = _in_proj_conv(xb, ssd_in_proj, conv_w, conv_b, j, seq, 0, D_INNER, GROUP_DIM, F32)
            b_t = _in_proj_conv(xb, ssd_in_proj, conv_w, conv_b, j, seq, D_INNER, D_BC, D_STATE, BF16,
                                transposed=True)
            cm = _in_proj_conv(xb, ssd_in_proj, conv_w, conv_b, j, seq, D_INNER + D_BC, D_BC, D_STATE, BF16)
            gy = _ssd_scan(xs, b_t, cm, z, dt_t, dt_bias, a_log, d_skip, norm_w, j, batch, seq)
            xf = _out_proj_norm(gy, w_out, xf, ln_mix_g, ln_mix_b, j, i)
        else:
            xf = _pool_mixer_norm(xf, pool_w, pool_b, pool_scale, ln_mix_g, ln_mix_b, j, i, seq)
        xf = _mlp_norm(xf, w1, mlp_w2, ln_ffn_g, ln_ffn_b, i)
    return xf.reshape(batch, seq, d)
```
